```python
import jax, jax.numpy as jnp
from jax import lax
import numpy as np

D_MODEL = 1024
BATCH = 2
SEQ = 8192
DEPTH = 4
DEC_BATCH = 128
DEC_SEQ = 8
PAST_LEN = 2048
PAGE_SIZE = 128

N_META = 16
M_HEADS = 4
M_V_DIM = D_MODEL // 8
M_QK_DIM = M_V_DIM // 2
M_CHUNK = 64
SB_HEADS = 8
SB_HEAD_DIM = D_MODEL // 16
SB_BLOCK = 128
SB_BIAS_INIT = -8.0
D_FF = 4 * D_MODEL
M_QK_W = M_HEADS * M_QK_DIM
M_V_W = M_HEADS * M_V_DIM
SB_W = SB_HEADS * SB_HEAD_DIM
FORGET_BIAS = 3.0
EPS = 1e-6
_SIZES = (M_QK_W, M_QK_W, M_V_W, M_V_W, M_HEADS, M_HEADS, SB_W, SB_W, SB_W, D_MODEL, D_MODEL)
N_IN = sum(_SIZES)
SPLIT_AT = tuple(sum(_SIZES[:i + 1]) for i in range(len(_SIZES) - 1))

kernel_name = 'mlstm_stickbreak_gated_hybrid_step'


def rms_norm(x, g):
    xf = x.astype(jnp.float32)
    y = xf * lax.rsqrt(jnp.mean(xf * xf, axis=-1, keepdims=True) + EPS)
    return (y * g.astype(jnp.float32)).astype(x.dtype)


def project(h, w_in, b_if, q_gain, k_gain):
    B, T = h.shape[:2]
    p = h @ w_in
    mq, mk, mv, mo, mi, mf, sq, sk, sv, ga, gb = jnp.split(p, SPLIT_AT, axis=-1)
    mq = mq.reshape(B, T, M_HEADS, M_QK_DIM)
    mk = mk.reshape(B, T, M_HEADS, M_QK_DIM) * (M_QK_DIM ** -0.5)
    mv = mv.reshape(B, T, M_HEADS, M_V_DIM)
    bf = b_if.astype(jnp.float32)
    logi = mi.astype(jnp.float32) + bf[:M_HEADS]
    logf = jax.nn.log_sigmoid(mf.astype(jnp.float32) + bf[M_HEADS:])
    sq = rms_norm(sq.reshape(B, T, SB_HEADS, SB_HEAD_DIM), q_gain)
    sk = rms_norm(sk.reshape(B, T, SB_HEADS, SB_HEAD_DIM), k_gain)
    sv = sv.reshape(B, T, SB_HEADS, SB_HEAD_DIM)
    return mq, mk, mv, mo, logi, logf, sq, sk, sv, ga, gb


def mlstm_chunk(carry, q, k, v, logi, logf):
    C0, n0, m0 = (c.astype(jnp.float32) for c in carry)
    q = jnp.swapaxes(q, 1, 2).astype(jnp.float32)
    k = jnp.swapaxes(k, 1, 2).astype(jnp.float32)
    v = jnp.swapaxes(v, 1, 2).astype(jnp.float32)
    logi = jnp.swapaxes(logi, 1, 2)
    b = jnp.cumsum(jnp.swapaxes(logf, 1, 2), axis=-1)
    L = q.shape[2]
    causal = jnp.tril(jnp.ones((L, L), dtype=bool))
    Dm = jnp.where(causal, b[..., :, None] - b[..., None, :] + logi[..., None, :], -jnp.inf)
    inter = b + m0[..., None]
    m = jnp.maximum(inter, jnp.max(Dm, axis=-1))
    S = jnp.einsum('bhtd,bhsd->bhts', q, k) * jnp.exp(Dm - m[..., None])
    e_inter = jnp.exp(inter - m)
    num = jnp.einsum('bhts,bhsv->bhtv', S, v) + e_inter[..., None] * jnp.einsum('bhvd,bhtd->bhtv', C0, q)
    den = jnp.sum(S, axis=-1) + e_inter * jnp.einsum('bhd,bhtd->bht', n0, q)
    h = num / jnp.maximum(jnp.abs(den), jnp.exp(-m))[..., None]
    m_new = m[..., -1]
    w = jnp.exp(b[..., -1:] - b + logi - m_new[..., None])
    decay = jnp.exp(inter[..., -1] - m_new)
    C_new = decay[..., None, None] * C0 + jnp.einsum('bhs,bhsv,bhsd->bhvd', w, v, k)
    n_new = decay[..., None] * n0 + jnp.einsum('bhs,bhsd->bhd', w, k)
    return (C_new, n_new, m_new), jnp.swapaxes(h, 1, 2)


def mlstm_prompt(q, k, v, logi, logf):
    B = q.shape[0]
    carry = (jnp.zeros((B, M_HEADS, M_V_DIM, M_QK_DIM), jnp.float32),
             jnp.zeros((B, M_HEADS, M_QK_DIM), jnp.float32),
             jnp.zeros((B, M_HEADS), jnp.float32))
    carry, h_meta = mlstm_chunk(carry, q[:, :N_META], k[:, :N_META], v[:, :N_META],
                                logi[:, :N_META], logf[:, :N_META])
    n_chunks = (q.shape[1] - N_META) // M_CHUNK

    def to_chunks(a):
        a = a[:, N_META:]
        return jnp.swapaxes(a.reshape(B, n_chunks, M_CHUNK, *a.shape[2:]), 0, 1)

    carry, h_real = lax.scan(lambda c, xs: mlstm_chunk(c, *xs), carry,
                             (to_chunks(q), to_chunks(k), to_chunks(v), to_chunks(logi), to_chunks(logf)))
    h_real = jnp.swapaxes(h_real, 0, 1).reshape(B, n_chunks * M_CHUNK, M_HEADS, M_V_DIM)
    return jnp.concatenate([h_meta, h_real], axis=1), carry


def stick_breaking(q, k, v, bias, q_pos, k_pos):
    z = jnp.einsum('bthd,bshd->bhts', q.astype(jnp.float32), k.astype(jnp.float32)) * (SB_HEAD_DIM ** -0.5)
    z = z + bias.astype(jnp.float32)[None, :, None, None]
    valid = k_pos[None, :] < q_pos[:, None]
    log_keep = jnp.where(valid, jax.nn.log_sigmoid(-z), 0.0)
    after = lax.cumsum(log_keep, axis=3, reverse=True) - log_keep
    A = jnp.where(valid, jnp.exp(jax.nn.log_sigmoid(z) + after), 0.0)
    return jnp.einsum('bhts,bshd->bthd', A, v.astype(jnp.float32)).astype(v.dtype)


def sb_prompt(q, k, v, bias):
    B, T = q.shape[:2]
    pos = jnp.arange(T)
    o_meta = stick_breaking(q[:, :N_META], k[:, :N_META], v[:, :N_META], bias, pos[:N_META], pos[:N_META])
    n_blocks = (T - N_META) // SB_BLOCK

    def block(j):
        start = N_META + j * SB_BLOCK
        qb = lax.dynamic_slice_in_dim(q, start, SB_BLOCK, axis=1)
        return stick_breaking(qb, k, v, bias, start + jnp.arange(SB_BLOCK), pos)

    o = lax.map(block, jnp.arange(n_blocks))
    o = jnp.swapaxes(o, 0, 1).reshape(B, n_blocks * SB_BLOCK, SB_HEADS, SB_HEAD_DIM)
    return jnp.concatenate([o_meta, o], axis=1)


def layer(x, norm1, w_in, b_if, q_gain, k_gain, sb_bias, mlstm_gain, w_br_a, w_br_b, w_o, norm2, w_up, w_down,
          mlstm_fn, sb_fn):
    B, T = x.shape[:2]
    h = rms_norm(x, norm1)
    mq, mk, mv, mo, logi, logf, sq, sk, sv, ga, gb = project(h, w_in, b_if, q_gain, k_gain)
    hm, mstate = mlstm_fn(mq, mk, mv, logi, logf)
    hs = sb_fn(sq, sk, sv, sb_bias)
    ya = (rms_norm(hm.astype(x.dtype), mlstm_gain).reshape(B, T, M_V_W) * jax.nn.sigmoid(mo)) @ w_br_a
    yb = hs.reshape(B, T, SB_W) @ w_br_b
    x = x + (jax.nn.sigmoid(ga) * ya + jax.nn.sigmoid(gb) * yb) @ w_o
    u = jax.nn.relu(rms_norm(x, norm2) @ w_up)
    x = x + (u * u) @ w_down
    return x, sk, sv, mstate


def setup_inputs(seed: int = 0) -> dict:
    key = jax.random.key(seed)
    ks = jax.random.split(key, 24)
    n_pages = PAST_LEN // PAGE_SIZE
    n_pool = (DEC_BATCH * n_pages * 5) // 4
    f32 = jnp.float32
    nrm = lambda k, shape, s: jax.random.normal(k, shape, f32) * s
    page_table = jax.random.permutation(ks[0], n_pool)[:DEC_BATCH * n_pages].reshape(DEC_BATCH, n_pages).astype(jnp.int32)
    b_if = jnp.concatenate([nrm(ks[1], (DEPTH, M_HEADS), 0.1),
                            FORGET_BIAS + nrm(ks[2], (DEPTH, M_HEADS), 0.1)], axis=-1)
    return {
        'x_prompt': nrm(ks[3], (BATCH, SEQ, D_MODEL), 1.0),
        'x_sample': nrm(ks[4], (DEC_BATCH, DEC_SEQ, D_MODEL), 1.0),
        'cache_k': nrm(ks[5], (DEPTH, n_pool, PAGE_SIZE, SB_HEADS, SB_HEAD_DIM), 1.0),
        'cache_v': nrm(ks[6], (DEPTH, n_pool, PAGE_SIZE, SB_HEADS, SB_HEAD_DIM), 1.0),
        'state_C': nrm(ks[7], (DEPTH, DEC_BATCH, M_HEADS, M_V_DIM, M_QK_DIM), 0.1),
        'state_n': nrm(ks[8], (DEPTH, DEC_BATCH, M_HEADS, M_QK_DIM), 0.1),
        'state_m': nrm(ks[9], (DEPTH, DEC_BATCH, M_HEADS), 0.5),
        'page_table': page_table,
        'meta_tokens': nrm(ks[10], (N_META, D_MODEL), 1.0),
        'norm1': 1.0 + nrm(ks[11], (DEPTH, D_MODEL), 0.02),
        'w_in': nrm(ks[12], (DEPTH, D_MODEL, N_IN), D_MODEL ** -0.5),
        'b_if': b_if,
        'sb_q_gain': 1.0 + nrm(ks[13], (DEPTH, SB_HEAD_DIM), 0.02),
        'sb_k_gain': 1.0 + nrm(ks[14], (DEPTH, SB_HEAD_DIM), 0.02),
        'sb_logit_bias': SB_BIAS_INIT + nrm(ks[22], (DEPTH, SB_HEADS), 0.5),
        'mlstm_gain': 1.0 + nrm(ks[15], (DEPTH, M_HEADS, M_V_DIM), 0.02),
        'w_br_a': nrm(ks[16], (DEPTH, M_V_W, D_MODEL), M_V_W ** -0.5),
        'w_br_b': nrm(ks[17], (DEPTH, SB_W, D_MODEL), SB_W ** -0.5),
        'w_o': nrm(ks[18], (DEPTH, D_MODEL, D_MODEL), D_MODEL ** -0.5),
        'norm2': 1.0 + nrm(ks[19], (DEPTH, D_MODEL), 0.02),
        'w_up': nrm(ks[20], (DEPTH, D_MODEL, D_FF), D_MODEL ** -0.5),
        'w_down': nrm(ks[21], (DEPTH, D_FF, D_MODEL), D_FF ** -0.5),
    }


def reference(x_prompt, x_sample, cache_k, cache_v, state_C, state_n, state_m, page_table, meta_tokens,
              norm1, w_in, b_if, sb_q_gain, sb_k_gain, sb_logit_bias, mlstm_gain, w_br_a, w_br_b, w_o, norm2,
              w_up, w_down):
    B = x_prompt.shape[0]
    meta = jnp.broadcast_to(meta_tokens.astype(x_prompt.dtype)[None], (B, N_META, D_MODEL))
    xp = jnp.concatenate([meta, x_prompt], axis=1)
    xs = x_sample
    n_dec, t_dec = xs.shape[:2]
    kp, vp, Cp, np_, mp, ksl, vsl, Cs, ns, ms = ([] for _ in range(10))
    for l in range(DEPTH):
        params = (norm1[l], w_in[l], b_if[l], sb_q_gain[l], sb_k_gain[l], sb_logit_bias[l], mlstm_gain[l],
                  w_br_a[l], w_br_b[l], w_o[l], norm2[l], w_up[l], w_down[l])
        xp, sk, sv, (C, n, m) = layer(xp, *params, mlstm_prompt, sb_prompt)
        kp.append(sk); vp.append(sv)
        Cp.append(C.astype(state_C.dtype)); np_.append(n.astype(state_n.dtype)); mp.append(m.astype(state_m.dtype))

        def mlstm_sample(q, k, v, logi, logf, l=l):
            carry, h = mlstm_chunk((state_C[l], state_n[l], state_m[l]), q, k, v, logi, logf)
            return h, carry

        def sb_sample(q, k, v, bias, l=l):
            k_past = cache_k[l][page_table].reshape(n_dec, -1, SB_HEADS, SB_HEAD_DIM).astype(k.dtype)
            v_past = cache_v[l][page_table].reshape(n_dec, -1, SB_HEADS, SB_HEAD_DIM).astype(v.dtype)
            past = k_past.shape[1]
            k_all = jnp.concatenate([k_past, k], axis=1)
            v_all = jnp.concatenate([v_past, v], axis=1)
            return stick_breaking(q, k_all, v_all, bias, past + jnp.arange(t_dec), jnp.arange(past + t_dec))

        xs, sk2, sv2, (C2, n2, m2) = layer(xs, *params, mlstm_sample, sb_sample)
        ksl.append(sk2); vsl.append(sv2)
        Cs.append(C2.astype(state_C.dtype)); ns.append(n2.astype(state_n.dtype)); ms.append(m2.astype(state_m.dtype))
    y_prompt = xp[:, N_META:]
    y_sample = xs
    k_prompt = jnp.stack(kp); v_prompt = jnp.stack(vp)
    C_prompt = jnp.stack(Cp); n_prompt = jnp.stack(np_); m_prompt = jnp.stack(mp)
    k_sample = jnp.stack(ksl); v_sample = jnp.stack(vsl)
    C_sample = jnp.stack(Cs); n_sample = jnp.stack(ns); m_sample = jnp.stack(ms)
    return (y_prompt, y_sample, k_prompt, v_prompt, C_prompt, n_prompt, m_prompt,
            k_sample, v_sample, C_sample, n_sample, m_sample)
```

```python
import functools

import jax
import jax.numpy as jnp
from jax import lax
from jax.experimental import pallas as pl
from jax.experimental.pallas import tpu as pltpu

F32 = jnp.float32
BF16 = jnp.bfloat16
HIGHEST = lax.Precision.HIGHEST

D_MODEL = 1024
N_META = 16
M_HEADS = 4
M_V_DIM = 128
M_QK_DIM = 64
SB_HEADS = 8
SB_HEAD_DIM = 64
D_FF = 4 * D_MODEL
M_QK_W = M_HEADS * M_QK_DIM
M_V_W = M_HEADS * M_V_DIM
SB_W = SB_HEADS * SB_HEAD_DIM
EPS = 1e-6
NEG = -1e30
GATE_W = 128

ROW_TILE = 256
M_CHUNK = 256
SB_BLOCK = 256
VMEM_LIMIT = 56 * 1024 * 1024

_C_MQ, _C_MK, _C_MV, _C_MO = 0, 256, 512, 1024
_C_SQ, _C_SK, _C_SV, _C_GA, _C_GB, _C_END = 1536, 2048, 2560, 3072, 4096, 5120


def _softplus(z):
    return jnp.maximum(z, 0.0) + jnp.log1p(jnp.exp(-jnp.abs(z)))


def _sigmoid(z):
    return 1.0 / (1.0 + jnp.exp(-z))


def _split_bf16(x):
    hi = x.astype(BF16)
    lo = (x - hi.astype(F32)).astype(BF16)
    return hi, lo


def _dot(a, b):
    return jnp.dot(a, b, preferred_element_type=F32)


def _dot_nt(a, b):
    return lax.dot_general(a, b, (((1,), (1,)), ((), ())), preferred_element_type=F32)


def _dot_tn(a, b):
    return lax.dot_general(a, b, (((0,), (0,)), ((), ())), preferred_element_type=F32)


def _head_norm(y, gain_full):
    y2 = y * y
    head = lax.broadcasted_iota(jnp.int32, (1, SB_W), 1) // SB_HEAD_DIM
    scale = jnp.zeros_like(y)
    for h in range(SB_HEADS):
        m = head == h
        ssq = jnp.sum(jnp.where(m, y2, 0.0), axis=-1, keepdims=True)
        r = lax.rsqrt(ssq * (1.0 / SB_HEAD_DIM) + EPS)
        scale = jnp.where(m, r, scale)
    return y * scale * gain_full


def _proj_kernel(x_ref, n1_ref, wm_ref, wg_ref, bg_ref, qg_ref, kg_ref,
                 mq_ref, mk_ref, mv_ref, mo_ref, g_ref, sq_ref, sk_ref, sv_ref, ga_ref, gb_ref):
    x = x_ref[...]
    ms = jnp.mean(x * x, axis=-1, keepdims=True)
    h = (x * lax.rsqrt(ms + EPS) * n1_ref[...]).astype(BF16)

    def mm(lo, hi):
        return _dot(h, wm_ref[:, lo:hi])

    mq_ref[...] = mm(_C_MQ, _C_MK)
    mk_ref[...] = mm(_C_MK, _C_MV) * (M_QK_DIM ** -0.5)
    mv_ref[...] = mm(_C_MV, _C_MO)
    mo_ref[...] = _sigmoid(mm(_C_MO, _C_SQ))
    g = _dot(h, wg_ref[...]) + bg_ref[...]
    lane = lax.broadcasted_iota(jnp.int32, (1, GATE_W), 1)
    g_ref[...] = jnp.where(lane < M_HEADS, g, -_softplus(-g))
    sq_ref[...] = _head_norm(mm(_C_SQ, _C_SK), qg_ref[...]) * (SB_HEAD_DIM ** -0.5)
    sk_ref[...] = _head_norm(mm(_C_SK, _C_SV), kg_ref[...])
    sv_ref[...] = mm(_C_SV, _C_GA)
    ga_ref[...] = _sigmoid(mm(_C_GA, _C_GB))
    gb_ref[...] = _sigmoid(mm(_C_GB, _C_END))


def _proj(x, n1, wm, wg, bg, qg, kg):
    rows = x.shape[0]
    tm = ROW_TILE
    widths = (M_QK_W, M_QK_W, M_V_W, M_V_W, GATE_W, SB_W, SB_W, SB_W, D_MODEL, D_MODEL)
    full = lambda a: pl.BlockSpec(a.shape, lambda i: (0,) * a.ndim)
    return pl.pallas_call(
        _proj_kernel,
        grid=(rows // tm,),
        in_specs=[pl.BlockSpec((tm, D_MODEL), lambda i: (i, 0)),
                  full(n1), full(wm), full(wg), full(bg), full(qg), full(kg)],
        out_specs=[pl.BlockSpec((tm, w), lambda i: (i, 0)) for w in widths],
        out_shape=[jax.ShapeDtypeStruct((rows, w), F32) for w in widths],
        compiler_params=pltpu.CompilerParams(dimension_semantics=("arbitrary",),
                                             vmem_limit_bytes=VMEM_LIMIT),
        name="proj",
    )(x, n1, wm, wg, bg, qg, kg)


def _mlstm_kernel(q_ref, k_ref, v_ref, g_ref, mo_ref, gain_ref, C0_ref, n0_ref, m0_ref,
                  a_ref, Co_ref, no_ref, mout_ref, C_scr, n_scr, m_scr, *, L, t_real, nc):
    c = pl.program_id(1)

    @pl.when(c == 0)
    def _():
        C_scr[...] = C0_ref[0]
        n_scr[...] = n0_ref[0]
        m_scr[...] = m0_ref[0]

    mxu = BF16 if L >= 128 else F32
    g = g_ref[...]
    row = lax.broadcasted_iota(jnp.int32, (L, 1), 0)
    lane = lax.broadcasted_iota(jnp.int32, (1, GATE_W), 1)
    valid = (c * L + row) < t_real
    g = jnp.where(valid, g, jnp.where(lane < M_HEADS, NEG, 0.0))
    ti = lax.broadcasted_iota(jnp.int32, (L, L), 0)
    si = lax.broadcasted_iota(jnp.int32, (L, L), 1)
    causal = si <= ti
    eye = si == ti
    b_all = jnp.dot(jnp.where(causal, 1.0, 0.0), g, precision=HIGHEST, preferred_element_type=F32)

    for h in range(M_HEADS):
        qf = q_ref[:, h * M_QK_DIM:(h + 1) * M_QK_DIM]
        kf = k_ref[:, h * M_QK_DIM:(h + 1) * M_QK_DIM]
        vf = v_ref[:, h * M_V_DIM:(h + 1) * M_V_DIM]
        q, k = qf.astype(mxu), kf.astype(mxu)
        C0 = C_scr[h]
        n0 = n_scr[h:h + 1, :]
        m0 = m_scr[0:1, h:h + 1]
        logi = g[:, h:h + 1]
        b = b_all[:, M_HEADS + h:M_HEADS + h + 1]
        r_col = logi - b
        r_row = jnp.sum(jnp.where(eye, r_col, 0.0), axis=0, keepdims=True)
        dm = jnp.where(causal, b + r_row, NEG)
        inter = b + m0
        m = jnp.maximum(inter, jnp.max(dm, axis=1, keepdims=True))
        s = _dot_nt(q, k) * jnp.exp(dm - m)
        e_inter = jnp.exp(inter - m)
        num = _dot(s.astype(mxu), vf.astype(mxu)) + e_inter * _dot_nt(q, C0.astype(mxu))
        den = jnp.sum(s, axis=1, keepdims=True) + e_inter * jnp.sum(qf * n0, axis=1, keepdims=True)
        hh = num / jnp.maximum(jnp.abs(den), jnp.exp(-m))
        hn = hh * lax.rsqrt(jnp.mean(hh * hh, axis=-1, keepdims=True) + EPS) * gain_ref[h:h + 1, :]
        a_ref[:, h * M_V_DIM:(h + 1) * M_V_DIM] = hn * mo_ref[:, h * M_V_DIM:(h + 1) * M_V_DIM]

        m_new = m[L - 1:L, :]
        b_last = b[L - 1:L, :]
        w = jnp.exp(b_last - b + logi - m_new)
        decay = jnp.exp(b_last + m0 - m_new)
        C_scr[h] = decay * C0 + _dot_tn((vf * w).astype(mxu), k)
        n_scr[h:h + 1, :] = decay * n0 + jnp.sum(w * kf, axis=0, keepdims=True)
        m_scr[0:1, h:h + 1] = m_new

    @pl.when(c == nc - 1)
    def _():
        Co_ref[0] = C_scr[...]
        no_ref[0] = n_scr[...]
        mout_ref[0] = m_scr[...]


def _mlstm(mq, mk, mv, g, mo, gain, C0, n0, m0, *, row0, n_seq, nc, L, t_real):
    blk0 = row0 // L
    rmap = lambda s, c: (blk0 + s * nc + c, 0)
    smap3 = lambda s, c: (s, 0, 0)
    smap4 = lambda s, c: (s, 0, 0, 0)
    m0 = m0.reshape(n_seq, 1, M_HEADS)
    kern = functools.partial(_mlstm_kernel, L=L, t_real=t_real, nc=nc)
    a, Co, no, mo_out = pl.pallas_call(
        kern,
        grid=(n_seq, nc),
        in_specs=[pl.BlockSpec((L, M_QK_W), rmap), pl.BlockSpec((L, M_QK_W), rmap),
                  pl.BlockSpec((L, M_V_W), rmap), pl.BlockSpec((L, GATE_W), rmap),
                  pl.BlockSpec((L, M_V_W), rmap),
                  pl.BlockSpec((M_HEADS, M_V_DIM), lambda s, c: (0, 0)),
                  pl.BlockSpec((1, M_HEADS, M_V_DIM, M_QK_DIM), smap4),
                  pl.BlockSpec((1, M_HEADS, M_QK_DIM), smap3),
                  pl.BlockSpec((1, 1, M_HEADS), smap3)],
        out_specs=[pl.BlockSpec((L, M_V_W), lambda s, c: (s * nc + c, 0)),
                   pl.BlockSpec((1, M_HEADS, M_V_DIM, M_QK_DIM), smap4),
                   pl.BlockSpec((1, M_HEADS, M_QK_DIM), smap3),
                   pl.BlockSpec((1, 1, M_HEADS), smap3)],
        out_shape=[jax.ShapeDtypeStruct((n_seq * nc * L, M_V_W), F32),
                   jax.ShapeDtypeStruct((n_seq, M_HEADS, M_V_DIM, M_QK_DIM), F32),
                   jax.ShapeDtypeStruct((n_seq, M_HEADS, M_QK_DIM), F32),
                   jax.ShapeDtypeStruct((n_seq, 1, M_HEADS), F32)],
        scratch_shapes=[pltpu.VMEM((M_HEADS, M_V_DIM, M_QK_DIM), F32),
                        pltpu.VMEM((M_HEADS, M_QK_DIM), F32),
                        pltpu.VMEM((1, M_HEADS), F32)],
        compiler_params=pltpu.CompilerParams(dimension_semantics=("arbitrary", "arbitrary"),
                                             vmem_limit_bytes=VMEM_LIMIT),
        name="mlstm_L%d" % L,
    )(mq, mk, mv, g, mo, gain, C0, n0, m0)
    return a, Co, no, mo_out.reshape(n_seq, M_HEADS)


def _sb_kernel(qi_ref, kj_ref, bias_ref, q_ref, k_ref, v_ref, o_ref, acc_ref, carry_ref, *, T):
    p = pl.program_id(1)
    i = qi_ref[p]
    j = kj_ref[p]

    @pl.when(j == i)
    def _():
        acc_ref[...] = jnp.zeros_like(acc_ref)
        carry_ref[...] = jnp.zeros_like(carry_ref)

    def body(diag):
        ti = lax.broadcasted_iota(jnp.int32, (T, T), 0)
        si = lax.broadcasted_iota(jnp.int32, (T, T), 1)
        upper = jnp.where(ti > si, 1.0, 0.0).astype(BF16)
        valid = si < ti
        for h in range(SB_HEADS):
            sl = slice(h * SB_HEAD_DIM, (h + 1) * SB_HEAD_DIM)
            q = q_ref[:, sl].astype(BF16)
            k = k_ref[:, sl].astype(BF16)
            v = v_ref[:, sl].astype(BF16)
            z = _dot_nt(q, k) + bias_ref[h]
            lk = -_softplus(z)
            if diag:
                lk = jnp.where(valid, lk, 0.0)
            hi, lo = _split_bf16(lk)
            after = _dot(hi, upper) + _dot(lo, upper) + carry_ref[h]
            a = jnp.exp(z + lk + after)
            if diag:
                a = jnp.where(valid, a, 0.0)
            acc_ref[:, sl] += _dot(a.astype(BF16), v)
            carry_ref[h] += jnp.sum(lk, axis=1, keepdims=True)

    @pl.when(j == i)
    def _():
        body(True)

    @pl.when(j < i)
    def _():
        body(False)

    @pl.when(j == 0)
    def _():
        o_ref[...] = acc_ref[...]


def _sb_prompt(sq, sk, sv, bias, *, n_batch, tp):
    T = SB_BLOCK
    nq = tp // T
    qi = jnp.asarray([i for i in range(nq) for _ in range(i + 1)], jnp.int32)
    kj = jnp.asarray([j for i in range(nq) for j in range(i, -1, -1)], jnp.int32)
    qmap = lambda b, p, qi, kj: (b * nq + qi[p], 0)
    kmap = lambda b, p, qi, kj: (b * nq + kj[p], 0)
    return pl.pallas_call(
        functools.partial(_sb_kernel, T=T),
        grid_spec=pltpu.PrefetchScalarGridSpec(
            num_scalar_prefetch=2,
            grid=(n_batch, int(qi.shape[0])),
            in_specs=[pl.BlockSpec(memory_space=pltpu.SMEM),
                      pl.BlockSpec((T, SB_W), qmap), pl.BlockSpec((T, SB_W), kmap),
                      pl.BlockSpec((T, SB_W), kmap)],
            out_specs=pl.BlockSpec((T, SB_W), qmap),
            scratch_shapes=[pltpu.VMEM((T, SB_W), F32), pltpu.VMEM((SB_HEADS, T, 1), F32)]),
        out_shape=jax.ShapeDtypeStruct((n_batch * tp, SB_W), F32),
        compiler_params=pltpu.CompilerParams(dimension_semantics=("arbitrary", "arbitrary"),
                                             vmem_limit_bytes=VMEM_LIMIT),
        name="sb_prompt",
    )(qi, kj, bias, sq, sk, sv)


def _sb_decode_kernel(pt_ref, bias_ref, q_ref, kc_ref, vc_ref, *refs, n_pages, page, t_dec):
    del pt_ref
    k_pages = refs[:n_pages]
    v_pages = refs[n_pages:2 * n_pages]
    o_ref = refs[2 * n_pages]
    nc = SB_HEADS * t_dec
    q = q_ref[...]
    crow = lax.broadcasted_iota(jnp.int32, (nc, SB_W), 0)
    clane = lax.broadcasted_iota(jnp.int32, (nc, SB_W), 1)
    own_head = (crow // t_dec) == (clane // SB_HEAD_DIM)
    q_bd = jnp.where(own_head, jnp.concatenate([q] * SB_HEADS, axis=0), 0.0).astype(BF16)
    bias = bias_ref[...]
    ji = lax.broadcasted_iota(jnp.int32, (page, page), 0)
    si = lax.broadcasted_iota(jnp.int32, (page, page), 1)
    upper = jnp.where(ji > si, 1.0, 0.0).astype(BF16)

    def block(z, valid, apply_v, carry, acc):
        z = z + bias
        lk = -_softplus(z)
        if valid is not None:
            lk = jnp.where(valid, lk, 0.0)
        hi, lo = _split_bf16(lk)
        after = _dot(hi, upper) + _dot(lo, upper) + carry
        a = jnp.exp(z + lk + after)
        if valid is not None:
            a = jnp.where(valid, a, 0.0)
        acc = acc + apply_v(a.astype(BF16))
        carry = carry + jnp.sum(lk, axis=1, keepdims=True)
        return carry, acc

    carry = jnp.zeros((nc, 1), F32)
    acc = jnp.zeros((nc, SB_W), F32)
    pad = jnp.zeros((page - t_dec, SB_W), F32)
    kc = jnp.concatenate([kc_ref[...], pad], axis=0).astype(BF16)
    vc = jnp.concatenate([vc_ref[...], pad], axis=0).astype(BF16)
    t_idx = lax.broadcasted_iota(jnp.int32, (nc, page), 0) % t_dec
    s_idx = lax.broadcasted_iota(jnp.int32, (nc, page), 1)
    carry, acc = block(_dot_nt(q_bd, kc), s_idx < t_idx, lambda a: _dot(a, vc), carry, acc)
    for pg in range(n_pages - 1, -1, -1):
        vt = v_pages[pg][...].astype(BF16)
        carry, acc = block(_dot(q_bd, k_pages[pg][...].astype(BF16)), None,
                           lambda a, vt=vt: _dot_nt(a, vt), carry, acc)

    olane = lax.broadcasted_iota(jnp.int32, (t_dec, SB_W), 1) // SB_HEAD_DIM
    out = jnp.zeros((t_dec, SB_W), F32)
    for h in range(SB_HEADS):
        out = out + jnp.where(olane == h, acc[h * t_dec:(h + 1) * t_dec, :], 0.0)
    o_ref[...] = out


def _sb_decode(sq, sk, sv, cache_k, cache_v, page_table, bias, *, layer, row0, t_dec):
    n_seq, n_pages = page_table.shape
    page = cache_k.shape[3]
    blk0 = row0 // t_dec
    nc = SB_HEADS * t_dec
    bias_row = jnp.repeat(bias, t_dec).reshape(nc, 1)
    pt = page_table.reshape(-1)
    rmap = lambda n, pt: (blk0 + n, 0)

    def pmap(pg):
        return lambda n, pt: (layer, pt[n * n_pages + pg], 0, 0)

    page_specs = [pl.BlockSpec((None, None, SB_W, page), pmap(pg)) for pg in range(n_pages)]
    kern = functools.partial(_sb_decode_kernel, n_pages=n_pages, page=page, t_dec=t_dec)
    return pl.pallas_call(
        kern,
        grid_spec=pltpu.PrefetchScalarGridSpec(
            num_scalar_prefetch=1,
            grid=(n_seq,),
            in_specs=[pl.BlockSpec((nc, 1), lambda n, pt: (0, 0)),
                      pl.BlockSpec((t_dec, SB_W), rmap), pl.BlockSpec((t_dec, SB_W), rmap),
                      pl.BlockSpec((t_dec, SB_W), rmap)] + page_specs + page_specs,
            out_specs=pl.BlockSpec((t_dec, SB_W), lambda n, pt: (n, 0))),
        out_shape=jax.ShapeDtypeStruct((n_seq * t_dec, SB_W), F32),
        compiler_params=pltpu.CompilerParams(dimension_semantics=("arbitrary",),
                                             vmem_limit_bytes=VMEM_LIMIT),
        name="sb_decode",
    )(pt, bias_row, sq, sk, sv, *([cache_k] * n_pages), *([cache_v] * n_pages))


def _merge_kernel(x_ref, a_ref, hs_ref, ga_ref, gb_ref, wa_ref, wb_ref, wo_ref, o_ref):
    ya = _dot(a_ref[...].astype(BF16), wa_ref[...])
    yb = _dot(hs_ref[...].astype(BF16), wb_ref[...])
    mix = ga_ref[...] * ya + gb_ref[...] * yb
    o_ref[...] = x_ref[...] + _dot(mix.astype(BF16), wo_ref[...])


def _merge(x, a, hs, ga, gb, wa, wb, wo):
    rows = x.shape[0]
    tm = ROW_TILE
    rspec = lambda w: pl.BlockSpec((tm, w), lambda i: (i, 0))
    full = lambda a: pl.BlockSpec(a.shape, lambda i: (0,) * a.ndim)
    return pl.pallas_call(
        _merge_kernel,
        grid=(rows // tm,),
        in_specs=[rspec(D_MODEL), rspec(M_V_W), rspec(SB_W), rspec(D_MODEL), rspec(D_MODEL),
                  full(wa), full(wb), full(wo)],
        out_specs=rspec(D_MODEL),
        out_shape=jax.ShapeDtypeStruct((rows, D_MODEL), F32),
        compiler_params=pltpu.CompilerParams(dimension_semantics=("arbitrary",),
                                             vmem_limit_bytes=VMEM_LIMIT),
        name="merge",
    )(x, a, hs, ga, gb, wa, wb, wo)


_FF_CHUNK = 1024


def _mlp_kernel(x_ref, n2_ref, wu_ref, wd_ref, o_ref):
    x = x_ref[...]
    ms = jnp.mean(x * x, axis=-1, keepdims=True)
    h = (x * lax.rsqrt(ms + EPS) * n2_ref[...]).astype(BF16)
    acc = x
    for c in range(D_FF // _FF_CHUNK):
        u = jnp.maximum(_dot(h, wu_ref[:, c * _FF_CHUNK:(c + 1) * _FF_CHUNK]), 0.0)
        acc = acc + _dot((u * u).astype(BF16), wd_ref[c * _FF_CHUNK:(c + 1) * _FF_CHUNK, :])
    o_ref[...] = acc


def _mlp(x, n2, wu, wd):
    rows = x.shape[0]
    tm = ROW_TILE
    full = lambda a: pl.BlockSpec(a.shape, lambda i: (0,) * a.ndim)
    return pl.pallas_call(
        _mlp_kernel,
        grid=(rows // tm,),
        in_specs=[pl.BlockSpec((tm, D_MODEL), lambda i: (i, 0)), full(n2), full(wu), full(wd)],
        out_specs=pl.BlockSpec((tm, D_MODEL), lambda i: (i, 0)),
        out_shape=jax.ShapeDtypeStruct((rows, D_MODEL), F32),
        compiler_params=pltpu.CompilerParams(dimension_semantics=("arbitrary",),
                                             vmem_limit_bytes=VMEM_LIMIT),
        name="mlp",
    )(x, n2, wu, wd)


def _round_up(n, m):
    return (n + m - 1) // m * m


def kernel(x_prompt, x_sample, cache_k, cache_v, state_C, state_n, state_m, page_table, meta_tokens,
           norm1, w_in, b_if, sb_q_gain, sb_k_gain, sb_logit_bias, mlstm_gain, w_br_a, w_br_b, w_o,
           norm2, w_up, w_down):
    n_batch, seq = x_prompt.shape[:2]
    n_dec, t_dec = x_sample.shape[:2]
    depth = w_in.shape[0]
    t_prompt = N_META + seq
    tp = _round_up(t_prompt, max(SB_BLOCK, M_CHUNK, ROW_TILE))
    n_sample_rows = n_dec * t_dec
    row_s = n_batch * tp
    rows = _round_up(row_s + n_sample_rows, ROW_TILE)

    meta = jnp.broadcast_to(meta_tokens[None], (n_batch, N_META, D_MODEL))
    xp = jnp.concatenate([meta, x_prompt, jnp.zeros((n_batch, tp - t_prompt, D_MODEL), F32)], axis=1)
    x = jnp.concatenate([xp.reshape(row_s, D_MODEL), x_sample.reshape(n_sample_rows, D_MODEL),
                         jnp.zeros((rows - row_s - n_sample_rows, D_MODEL), F32)], axis=0)

    n_pool, page = cache_k.shape[1:3]
    cache_k = jnp.transpose(cache_k, (0, 1, 3, 4, 2)).reshape(depth, n_pool, SB_W, page)
    cache_v = jnp.transpose(cache_v, (0, 1, 3, 4, 2)).reshape(depth, n_pool, SB_W, page)
    c_gate = 2 * M_QK_W + 2 * M_V_W
    zeros_C = jnp.zeros((n_batch, M_HEADS, M_V_DIM, M_QK_DIM), F32)
    zeros_n = jnp.zeros((n_batch, M_HEADS, M_QK_DIM), F32)
    zeros_m = jnp.zeros((n_batch, M_HEADS), F32)
    tail = jnp.zeros((rows - row_s - n_sample_rows, SB_W), F32)

    outs = [[] for _ in range(10)]
    for l in range(depth):
        wm = jnp.concatenate([w_in[l, :, :c_gate], w_in[l, :, c_gate + 2 * M_HEADS:]], axis=1).astype(BF16)
        wg = jnp.pad(w_in[l, :, c_gate:c_gate + 2 * M_HEADS], ((0, 0), (0, GATE_W - 2 * M_HEADS))).astype(BF16)
        bg = jnp.pad(b_if[l], (0, GATE_W - 2 * M_HEADS)).reshape(1, GATE_W)
        qg = jnp.tile(sb_q_gain[l], SB_HEADS).reshape(1, SB_W)
        kg = jnp.tile(sb_k_gain[l], SB_HEADS).reshape(1, SB_W)
        mq, mk, mv, mo, g, sq, sk, sv, ga, gb = _proj(x, norm1[l].reshape(1, D_MODEL), wm, wg, bg, qg, kg)

        a_p, C_p, n_p, m_p = _mlstm(mq, mk, mv, g, mo, mlstm_gain[l], zeros_C, zeros_n, zeros_m,
                                    row0=0, n_seq=n_batch, nc=tp // M_CHUNK, L=M_CHUNK, t_real=t_prompt)
        a_s, C_s, n_s, m_s = _mlstm(mq, mk, mv, g, mo, mlstm_gain[l], state_C[l], state_n[l], state_m[l],
                                    row0=row_s, n_seq=n_dec, nc=1, L=t_dec, t_real=t_dec)
        hs_p = _sb_prompt(sq, sk, sv, sb_logit_bias[l], n_batch=n_batch, tp=tp)
        hs_s = _sb_decode(sq, sk, sv, cache_k, cache_v, page_table, sb_logit_bias[l],
                          layer=l, row0=row_s, t_dec=t_dec)
        a = jnp.concatenate([a_p, a_s, tail], axis=0)
        hs = jnp.concatenate([hs_p, hs_s, tail], axis=0)
        x = _merge(x, a, hs, ga, gb, w_br_a[l].astype(BF16), w_br_b[l].astype(BF16), w_o[l].astype(BF16))
        x = _mlp(x, norm2[l].reshape(1, D_MODEL), w_up[l].astype(BF16), w_down[l].astype(BF16))

        kv_p = lambda t: t[:row_s].reshape(n_batch, tp, SB_HEADS, SB_HEAD_DIM)[:, :t_prompt]
        kv_s = lambda t: t[row_s:row_s + n_sample_rows].reshape(n_dec, t_dec, SB_HEADS, SB_HEAD_DIM)
        for lst, val in zip(outs, (kv_p(sk), kv_p(sv), C_p, n_p, m_p, kv_s(sk), kv_s(sv), C_s, n_s, m_s)):
            lst.append(val)

    y_prompt = x[:row_s].reshape(n_batch, tp, D_MODEL)[:, N_META:t_prompt]
    y_sample = x[row_s:row_s + n_sample_rows].reshape(n_dec, t_dec, D_MODEL)
    return (y_prompt, y_sample) + tuple(jnp.stack(o) for o in outs)
```

```python
import functools

import jax
import jax.numpy as jnp
from jax import lax
from jax.experimental import pallas as pl
from jax.experimental.pallas import tpu as pltpu

F32 = jnp.float32
BF16 = jnp.bfloat16
HIGHEST = lax.Precision.HIGHEST

D_MODEL = 1024
N_META = 16
M_HEADS = 4
M_V_DIM = 128
M_QK_DIM = 64
SB_HEADS = 8
SB_HEAD_DIM = 64
D_FF = 4 * D_MODEL
M_QK_W = M_HEADS * M_QK_DIM
M_V_W = M_HEADS * M_V_DIM
SB_W = SB_HEADS * SB_HEAD_DIM
EPS = 1e-6
NEG = -1e30
LOG2E = 1.4426950408889634
GATE_W = 128
PAIR_W = 2 * SB_HEAD_DIM

ROW_TILE = 256
M_CHUNK = 256
SB_BLOCK = 256
SB_STAGGER = 1
VMEM_LIMIT = 56 * 1024 * 1024

_C_MQ, _C_MK, _C_MV, _C_MO = 0, 256, 512, 1024
_C_SQ, _C_SK, _C_SV, _C_GA, _C_GB, _C_END = 1536, 2048, 2560, 3072, 4096, 5120


def _softplus(z):
    return jnp.maximum(z, 0.0) + jnp.log1p(jnp.exp(-jnp.abs(z)))


def _softplus2(z):
    neg_abs = pltpu.bitcast(pltpu.bitcast(z, jnp.uint32) | jnp.uint32(0x80000000), F32)
    return jnp.maximum(z, 0.0) + jnp.log(1.0 + jnp.exp2(neg_abs)) * LOG2E


def _sigmoid(z):
    return 1.0 / (1.0 + jnp.exp(-z))


def _split_bf16(x):
    hi = x.astype(BF16)
    lo = (x - hi.astype(F32)).astype(BF16)
    return hi, lo


def _dot(a, b):
    return jnp.dot(a, b, preferred_element_type=F32)


def _dot_nt(a, b):
    return lax.dot_general(a, b, (((1,), (1,)), ((), ())), preferred_element_type=F32)


def _dot_tn(a, b):
    return lax.dot_general(a, b, (((0,), (0,)), ((), ())), preferred_element_type=F32)


def _head_norm(y, gain_full):
    y2 = y * y
    head = lax.broadcasted_iota(jnp.int32, (1, SB_W), 1) // SB_HEAD_DIM
    scale = jnp.zeros_like(y)
    for h in range(SB_HEADS):
        m = head == h
        ssq = jnp.sum(jnp.where(m, y2, 0.0), axis=-1, keepdims=True)
        r = lax.rsqrt(ssq * (1.0 / SB_HEAD_DIM) + EPS)
        scale = jnp.where(m, r, scale)
    return y * scale * gain_full


def _proj_kernel(x_ref, n1_ref, wm_ref, wg_ref, bg_ref, qg_ref, kg_ref,
                 mq_ref, mk_ref, mv_ref, mo_ref, g_ref, sq_ref, sk_ref, sv_ref, ga_ref, gb_ref,
                 sqb_ref, skb_ref, svb_ref):
    x = x_ref[...]
    ms = jnp.mean(x * x, axis=-1, keepdims=True)
    h = (x * lax.rsqrt(ms + EPS) * n1_ref[...]).astype(BF16)

    def mm(lo, hi):
        return _dot(h, wm_ref[:, lo:hi])

    mq_ref[...] = mm(_C_MQ, _C_MK)
    mk_ref[...] = mm(_C_MK, _C_MV) * (M_QK_DIM ** -0.5)
    mv_ref[...] = mm(_C_MV, _C_MO)
    mo_ref[...] = _sigmoid(mm(_C_MO, _C_SQ))
    g = _dot(h, wg_ref[...]) + bg_ref[...]
    lane = lax.broadcasted_iota(jnp.int32, (1, GATE_W), 1)
    g_ref[...] = jnp.where(lane < M_HEADS, g, -_softplus(-g))
    sq = _head_norm(mm(_C_SQ, _C_SK), qg_ref[...]) * (SB_HEAD_DIM ** -0.5 * LOG2E)
    sk = _head_norm(mm(_C_SK, _C_SV), kg_ref[...])
    sv = mm(_C_SV, _C_GA)
    sq_ref[...], sk_ref[...], sv_ref[...] = sq, sk, sv
    skb_ref[...] = sk.astype(BF16)
    first = lax.broadcasted_iota(jnp.int32, (1, PAIR_W), 1) < SB_HEAD_DIM
    for src, dst in ((sq, sqb_ref), (sv, svb_ref)):
        for head in range(SB_HEADS):
            pair = src[:, (head // 2) * PAIR_W:(head // 2 + 1) * PAIR_W]
            own = first if head % 2 == 0 else jnp.logical_not(first)
            dst[:, head * PAIR_W:(head + 1) * PAIR_W] = jnp.where(own, pair, 0.0).astype(BF16)
    ga_ref[...] = _sigmoid(mm(_C_GA, _C_GB))
    gb_ref[...] = _sigmoid(mm(_C_GB, _C_END))


def _proj(x, n1, wm, wg, bg, qg, kg):
    rows = x.shape[0]
    tm = ROW_TILE
    widths = (M_QK_W, M_QK_W, M_V_W, M_V_W, GATE_W, SB_W, SB_W, SB_W, D_MODEL, D_MODEL)
    bf_widths = (SB_HEADS * PAIR_W, SB_W, SB_HEADS * PAIR_W)
    full = lambda a: pl.BlockSpec(a.shape, lambda i: (0,) * a.ndim)
    return pl.pallas_call(
        _proj_kernel,
        grid=(rows // tm,),
        in_specs=[pl.BlockSpec((tm, D_MODEL), lambda i: (i, 0)),
                  full(n1), full(wm), full(wg), full(bg), full(qg), full(kg)],
        out_specs=[pl.BlockSpec((tm, w), lambda i: (i, 0)) for w in widths + bf_widths],
        out_shape=[jax.ShapeDtypeStruct((rows, w), F32) for w in widths]
        + [jax.ShapeDtypeStruct((rows, w), BF16) for w in bf_widths],
        compiler_params=pltpu.CompilerParams(dimension_semantics=("arbitrary",),
                                             vmem_limit_bytes=VMEM_LIMIT),
        name="proj",
    )(x, n1, wm, wg, bg, qg, kg)


def _mlstm_kernel(q_ref, k_ref, v_ref, g_ref, mo_ref, gain_ref, C0_ref, n0_ref, m0_ref,
                  a_ref, Co_ref, no_ref, mout_ref, C_scr, n_scr, m_scr, *, L, t_real, nc, G):
    c = pl.program_id(1)

    @pl.when(c == 0)
    def _():
        C_scr[...] = C0_ref[...]
        n_scr[...] = n0_ref[...]
        m_scr[...] = m0_ref[...]

    mxu = BF16 if L >= 128 else F32
    row = lax.broadcasted_iota(jnp.int32, (L, 1), 0)
    lane = lax.broadcasted_iota(jnp.int32, (1, GATE_W), 1)
    valid = (c * L + row) < t_real
    ti = lax.broadcasted_iota(jnp.int32, (L, L), 0)
    si = lax.broadcasted_iota(jnp.int32, (L, L), 1)
    causal = si <= ti
    eye = si == ti

    for seq, h in [(seq, h) for seq in range(G) for h in range(M_HEADS)]:
        rs = slice(seq * L, (seq + 1) * L)
        if h == 0:
            g = jnp.where(valid, g_ref[rs, :], jnp.where(lane < M_HEADS, NEG, 0.0))
            b_all = jnp.dot(jnp.where(causal, 1.0, 0.0), g, precision=HIGHEST, preferred_element_type=F32)
        qf = q_ref[rs, h * M_QK_DIM:(h + 1) * M_QK_DIM]
        kf = k_ref[rs, h * M_QK_DIM:(h + 1) * M_QK_DIM]
        vf = v_ref[rs, h * M_V_DIM:(h + 1) * M_V_DIM]
        q, k = qf.astype(mxu), kf.astype(mxu)
        C0 = C_scr[seq, h]
        n0 = n_scr[seq, h:h + 1, :]
        m0 = m_scr[seq, 0:1, h:h + 1]
        logi = g[:, h:h + 1]
        b = b_all[:, M_HEADS + h:M_HEADS + h + 1]
        r_col = logi - b
        r_row = jnp.sum(jnp.where(eye, r_col, 0.0), axis=0, keepdims=True)
        dm = jnp.where(causal, b + r_row, NEG)
        inter = b + m0
        m = jnp.maximum(inter, jnp.max(dm, axis=1, keepdims=True))
        s = _dot_nt(q, k) * jnp.exp(dm - m)
        e_inter = jnp.exp(inter - m)
        num = _dot(s.astype(mxu), vf.astype(mxu)) + e_inter * _dot_nt(q, C0.astype(mxu))
        den = jnp.sum(s, axis=1, keepdims=True) + e_inter * jnp.sum(qf * n0, axis=1, keepdims=True)
        hh = num / jnp.maximum(jnp.abs(den), jnp.exp(-m))
        hn = hh * lax.rsqrt(jnp.mean(hh * hh, axis=-1, keepdims=True) + EPS) * gain_ref[h:h + 1, :]
        a_ref[rs, h * M_V_DIM:(h + 1) * M_V_DIM] = hn * mo_ref[rs, h * M_V_DIM:(h + 1) * M_V_DIM]

        m_new = m[L - 1:L, :]
        b_last = b[L - 1:L, :]
        w = jnp.exp(b_last - b + logi - m_new)
        decay = jnp.exp(b_last + m0 - m_new)
        C_scr[seq, h] = decay * C0 + _dot_tn((vf * w).astype(mxu), k)
        n_scr[seq, h:h + 1, :] = decay * n0 + jnp.sum(w * kf, axis=0, keepdims=True)
        m_scr[seq, 0:1, h:h + 1] = m_new

    @pl.when(c == nc - 1)
    def _():
        Co_ref[...] = C_scr[...]
        no_ref[...] = n_scr[...]
        mout_ref[...] = m_scr[...]


def _mlstm(mq, mk, mv, g, mo, gain, C0, n0, m0, *, row0, n_seq, nc, L, t_real):
    G = 1 if nc > 1 else max(d for d in (1, 2, 4, 8) if n_seq % d == 0)
    blk0 = row0 // (G * L)
    rmap = lambda s, c: (blk0 + s * nc + c, 0)
    smap3 = lambda s, c: (s, 0, 0)
    smap4 = lambda s, c: (s, 0, 0, 0)
    m0 = m0.reshape(n_seq, 1, M_HEADS)
    kern = functools.partial(_mlstm_kernel, L=L, t_real=t_real, nc=nc, G=G)
    a, Co, no, mo_out = pl.pallas_call(
        kern,
        grid=(n_seq // G, nc),
        in_specs=[pl.BlockSpec((G * L, M_QK_W), rmap), pl.BlockSpec((G * L, M_QK_W), rmap),
                  pl.BlockSpec((G * L, M_V_W), rmap), pl.BlockSpec((G * L, GATE_W), rmap),
                  pl.BlockSpec((G * L, M_V_W), rmap),
                  pl.BlockSpec((M_HEADS, M_V_DIM), lambda s, c: (0, 0)),
                  pl.BlockSpec((G, M_HEADS, M_V_DIM, M_QK_DIM), smap4),
                  pl.BlockSpec((G, M_HEADS, M_QK_DIM), smap3),
                  pl.BlockSpec((G, 1, M_HEADS), smap3)],
        out_specs=[pl.BlockSpec((G * L, M_V_W), lambda s, c: (s * nc + c, 0)),
                   pl.BlockSpec((G, M_HEADS, M_V_DIM, M_QK_DIM), smap4),
                   pl.BlockSpec((G, M_HEADS, M_QK_DIM), smap3),
                   pl.BlockSpec((G, 1, M_HEADS), smap3)],
        out_shape=[jax.ShapeDtypeStruct((n_seq * nc * L, M_V_W), F32),
                   jax.ShapeDtypeStruct((n_seq, M_HEADS, M_V_DIM, M_QK_DIM), F32),
                   jax.ShapeDtypeStruct((n_seq, M_HEADS, M_QK_DIM), F32),
                   jax.ShapeDtypeStruct((n_seq, 1, M_HEADS), F32)],
        scratch_shapes=[pltpu.VMEM((G, M_HEADS, M_V_DIM, M_QK_DIM), F32),
                        pltpu.VMEM((G, M_HEADS, M_QK_DIM), F32),
                        pltpu.VMEM((G, 1, M_HEADS), F32)],
        compiler_params=pltpu.CompilerParams(dimension_semantics=("arbitrary", "arbitrary"),
                                             vmem_limit_bytes=VMEM_LIMIT),
        name="mlstm_L%d" % L,
    )(mq, mk, mv, g, mo, gain, C0, n0, m0)
    return a, Co, no, mo_out.reshape(n_seq, M_HEADS)


def _sb_kernel(qi_ref, k1_ref, k2_ref, bias_ref, q_ref, ka_ref, va_ref, kb_ref, vb_ref, o_ref,
               acc_ref, carry_ref, *, T):
    p = pl.program_id(1)
    i = qi_ref[p]
    j1 = k1_ref[p]
    j2 = k2_ref[p]

    @pl.when(j1 == i)
    def _():
        acc_ref[...] = jnp.zeros_like(acc_ref)
        carry_ref[...] = jnp.zeros_like(carry_ref)

    def body(diag, two):
        ti = lax.broadcasted_iota(jnp.int32, (T, T), 0)
        si = lax.broadcasted_iota(jnp.int32, (T, T), 1)
        upper = jnp.where(ti > si, 1.0, 0.0).astype(BF16)
        valid = si < ti
        blocks = [(ka_ref, va_ref, diag)] + ([(kb_ref, vb_ref, False)] if two else [])
        units = [(h, blk) for h in range(SB_HEADS) for blk in range(len(blocks))]
        sls = [slice((h // 2) * PAIR_W, (h // 2 + 1) * PAIR_W) for h in range(SB_HEADS)]
        own = [slice(h * PAIR_W, (h + 1) * PAIR_W) for h in range(SB_HEADS)]
        zs, rest, behind = {}, {}, {}
        carries = [carry_ref[h] for h in range(SB_HEADS)]
        outs = [None] * SB_HEADS

        def scores(h, blk):
            zs[h, blk] = _dot_nt(q_ref[:, own[h]], blocks[blk][0][:, sls[h]]) + bias_ref[h]

        def sums(h, blk):
            z = zs.pop((h, blk))
            sp = _softplus2(z)
            if blocks[blk][2]:
                sp = jnp.where(valid, sp, 0.0)
            behind[h, blk] = _dot(sp.astype(BF16), upper) + jnp.concatenate([carries[h]] * (T // PAIR_W), axis=1)
            rest[h, blk] = z - sp
            carries[h] = carries[h] + jnp.sum(sp, axis=1, keepdims=True)

        def weights(h, blk):
            a = jnp.exp2(rest.pop((h, blk)) - behind.pop((h, blk)))
            if blocks[blk][2]:
                a = jnp.where(valid, a, 0.0)
            o = _dot(a.astype(BF16), blocks[blk][1][:, own[h]])
            outs[h] = o if outs[h] is None else outs[h] + o

        for s in range(len(units) + 2 * SB_STAGGER):
            if s < len(units):
                scores(*units[s])
            if 0 <= s - SB_STAGGER < len(units):
                sums(*units[s - SB_STAGGER])
            if 0 <= s - 2 * SB_STAGGER < len(units):
                weights(*units[s - 2 * SB_STAGGER])
        for h in range(SB_HEADS):
            carry_ref[h] = carries[h]
        for pair in range(SB_HEADS // 2):
            acc_ref[:, sls[2 * pair]] += outs[2 * pair] + outs[2 * pair + 1]

    for diag in (True, False):
        for two in (True, False):
            cond = ((j1 == i) if diag else (j1 < i)) & ((j2 >= 0) if two else (j2 < 0))
            pl.when(cond)(functools.partial(body, diag, two))

    @pl.when((j1 == 0) | (j2 == 0))
    def _():
        o_ref[...] = acc_ref[...]


def _sb_prompt(sq, sk, sv, bias, *, n_batch, tp):
    T = SB_BLOCK
    nq = tp // T
    steps = [(i, j, j - 1 if j >= 1 else -1) for i in range(nq) for j in range(i, -1, -2)]
    qi, k1, k2 = (jnp.asarray(c, jnp.int32) for c in zip(*steps))
    qmap = lambda b, p, qi, k1, k2: (b * nq + qi[p], 0)
    amap = lambda b, p, qi, k1, k2: (b * nq + k1[p], 0)
    bmap = lambda b, p, qi, k1, k2: (b * nq + jnp.maximum(k2[p], 0), 0)
    return pl.pallas_call(
        functools.partial(_sb_kernel, T=T),
        grid_spec=pltpu.PrefetchScalarGridSpec(
            num_scalar_prefetch=3,
            grid=(n_batch, len(steps)),
            in_specs=[pl.BlockSpec(memory_space=pltpu.SMEM),
                      pl.BlockSpec((T, SB_HEADS * PAIR_W), qmap),
                      pl.BlockSpec((T, SB_W), amap), pl.BlockSpec((T, SB_HEADS * PAIR_W), amap),
                      pl.BlockSpec((T, SB_W), bmap), pl.BlockSpec((T, SB_HEADS * PAIR_W), bmap)],
            out_specs=pl.BlockSpec((T, SB_W), qmap),
            scratch_shapes=[pltpu.VMEM((T, SB_W), F32), pltpu.VMEM((SB_HEADS, T, PAIR_W), F32)]),
        out_shape=jax.ShapeDtypeStruct((n_batch * tp, SB_W), F32),
        compiler_params=pltpu.CompilerParams(dimension_semantics=("arbitrary", "arbitrary"),
                                             vmem_limit_bytes=VMEM_LIMIT),
        name="sb_prompt",
    )(qi, k1, k2, bias, sq, sk, sv, sk, sv)


def _sb_decode_kernel(pt_ref, bias_ref, q_ref, kc_ref, vc_ref, *refs, n_pages, page, t_dec):
    del pt_ref
    k_pages = refs[:n_pages]
    v_pages = refs[n_pages:2 * n_pages]
    o_ref = refs[2 * n_pages]
    nc = SB_HEADS * t_dec
    q = q_ref[...]
    crow = lax.broadcasted_iota(jnp.int32, (nc, SB_W), 0)
    clane = lax.broadcasted_iota(jnp.int32, (nc, SB_W), 1)
    own_head = (crow // t_dec) == (clane // SB_HEAD_DIM)
    q_bd = jnp.where(own_head, jnp.concatenate([q] * SB_HEADS, axis=0), 0.0).astype(BF16)
    bias = bias_ref[...]
    ji = lax.broadcasted_iota(jnp.int32, (page, page), 0)
    si = lax.broadcasted_iota(jnp.int32, (page, page), 1)
    upper = jnp.where(ji > si, 1.0, 0.0).astype(BF16)

    pad = jnp.zeros((page - t_dec, SB_W), F32)
    kc = jnp.concatenate([kc_ref[...], pad], axis=0).astype(BF16)
    vc = jnp.concatenate([vc_ref[...], pad], axis=0).astype(BF16)
    t_idx = lax.broadcasted_iota(jnp.int32, (nc, page), 0) % t_dec
    s_idx = lax.broadcasted_iota(jnp.int32, (nc, page), 1)
    valid = s_idx < t_idx
    order = list(range(n_pages - 1, -1, -1))
    zs = [_dot_nt(q_bd, kc) + bias] + [_dot(q_bd, k_pages[pg][...].astype(BF16)) + bias for pg in order]
    rests, behinds = [], []
    carry = jnp.zeros((nc, 1), F32)
    for n, z in enumerate(zs):
        sp = _softplus2(z)
        if n == 0:
            sp = jnp.where(valid, sp, 0.0)
        behinds.append(_dot(sp.astype(BF16), upper) + carry)
        rests.append(z - sp)
        carry = carry + jnp.sum(sp, axis=1, keepdims=True)
    acc = jnp.zeros((nc, SB_W), F32)
    for n, (rest, behind) in enumerate(zip(rests, behinds)):
        a = jnp.exp2(rest - behind)
        if n == 0:
            a = jnp.where(valid, a, 0.0).astype(BF16)
            acc = acc + _dot(a, vc)
        else:
            acc = acc + _dot_nt(a.astype(BF16), v_pages[order[n - 1]][...].astype(BF16))

    olane = lax.broadcasted_iota(jnp.int32, (t_dec, SB_W), 1) // SB_HEAD_DIM
    out = jnp.zeros((t_dec, SB_W), F32)
    for h in range(SB_HEADS):
        out = out + jnp.where(olane == h, acc[h * t_dec:(h + 1) * t_dec, :], 0.0)
    o_ref[...] = out


def _sb_decode(sq, sk, sv, cache_k, cache_v, page_table, bias, *, layer, row0, t_dec):
    n_seq, n_pages = page_table.shape
    page = cache_k.shape[3]
    blk0 = row0 // t_dec
    nc = SB_HEADS * t_dec
    bias_row = jnp.repeat(bias, t_dec).reshape(nc, 1)
    pt = page_table.reshape(-1)
    rmap = lambda n, pt: (blk0 + n, 0)

    def pmap(pg):
        return lambda n, pt: (layer, pt[n * n_pages + pg], 0, 0)

    page_specs = [pl.BlockSpec((None, None, SB_W, page), pmap(pg)) for pg in range(n_pages)]
    kern = functools.partial(_sb_decode_kernel, n_pages=n_pages, page=page, t_dec=t_dec)
    return pl.pallas_call(
        kern,
        grid_spec=pltpu.PrefetchScalarGridSpec(
            num_scalar_prefetch=1,
            grid=(n_seq,),
            in_specs=[pl.BlockSpec((nc, 1), lambda n, pt: (0, 0)),
                      pl.BlockSpec((t_dec, SB_W), rmap), pl.BlockSpec((t_dec, SB_W), rmap),
                      pl.BlockSpec((t_dec, SB_W), rmap)] + page_specs + page_specs,
            out_specs=pl.BlockSpec((t_dec, SB_W), lambda n, pt: (n, 0))),
        out_shape=jax.ShapeDtypeStruct((n_seq * t_dec, SB_W), F32),
        compiler_params=pltpu.CompilerParams(dimension_semantics=("arbitrary",),
                                             vmem_limit_bytes=VMEM_LIMIT),
        name="sb_decode",
    )(pt, bias_row, sq, sk, sv, *([cache_k] * n_pages), *([cache_v] * n_pages))


def _merge_kernel(x_ref, a_ref, hs_ref, ga_ref, gb_ref, wa_ref, wb_ref, wo_ref, o_ref):
    ya = _dot(a_ref[...].astype(BF16), wa_ref[...])
    yb = _dot(hs_ref[...].astype(BF16), wb_ref[...])
    mix = ga_ref[...] * ya + gb_ref[...] * yb
    o_ref[...] = x_ref[...] + _dot(mix.astype(BF16), wo_ref[...])


def _merge(x, a, hs, ga, gb, wa, wb, wo):
    rows = x.shape[0]
    tm = ROW_TILE
    rspec = lambda w: pl.BlockSpec((tm, w), lambda i: (i, 0))
    full = lambda a: pl.BlockSpec(a.shape, lambda i: (0,) * a.ndim)
    return pl.pallas_call(
        _merge_kernel,
        grid=(rows // tm,),
        in_specs=[rspec(D_MODEL), rspec(M_V_W), rspec(SB_W), rspec(D_MODEL), rspec(D_MODEL),
                  full(wa), full(wb), full(wo)],
        out_specs=rspec(D_MODEL),
        out_shape=jax.ShapeDtypeStruct((rows, D_MODEL), F32),
        compiler_params=pltpu.CompilerParams(dimension_semantics=("arbitrary",),
                                             vmem_limit_bytes=VMEM_LIMIT),
        name="merge",
    )(x, a, hs, ga, gb, wa, wb, wo)


_FF_CHUNK = 1024


def _mlp_kernel(x_ref, n2_ref, wu_ref, wd_ref, o_ref):
    x = x_ref[...]
    ms = jnp.mean(x * x, axis=-1, keepdims=True)
    h = (x * lax.rsqrt(ms + EPS) * n2_ref[...]).astype(BF16)
    acc = x
    for c in range(D_FF // _FF_CHUNK):
        u = jnp.maximum(_dot(h, wu_ref[:, c * _FF_CHUNK:(c + 1) * _FF_CHUNK]), 0.0)
        acc = acc + _dot((u * u).astype(BF16), wd_ref[c * _FF_CHUNK:(c + 1) * _FF_CHUNK, :])
    o_ref[...] = acc


def _mlp(x, n2, wu, wd):
    rows = x.shape[0]
    tm = ROW_TILE
    full = lambda a: pl.BlockSpec(a.shape, lambda i: (0,) * a.ndim)
    return pl.pallas_call(
        _mlp_kernel,
        grid=(rows // tm,),
        in_specs=[pl.BlockSpec((tm, D_MODEL), lambda i: (i, 0)), full(n2), full(wu), full(wd)],
        out_specs=pl.BlockSpec((tm, D_MODEL), lambda i: (i, 0)),
        out_shape=jax.ShapeDtypeStruct((rows, D_MODEL), F32),
        compiler_params=pltpu.CompilerParams(dimension_semantics=("arbitrary",),
                                             vmem_limit_bytes=VMEM_LIMIT),
        name="mlp",
    )(x, n2, wu, wd)


def _round_up(n, m):
    return (n + m - 1) // m * m


def kernel(x_prompt, x_sample, cache_k, cache_v, state_C, state_n, state_m, page_table, meta_tokens,
           norm1, w_in, b_if, sb_q_gain, sb_k_gain, sb_logit_bias, mlstm_gain, w_br_a, w_br_b, w_o,
           norm2, w_up, w_down):
    n_batch, seq = x_prompt.shape[:2]
    n_dec, t_dec = x_sample.shape[:2]
    depth = w_in.shape[0]
    t_prompt = N_META + seq
    tp = _round_up(t_prompt, max(SB_BLOCK, M_CHUNK, ROW_TILE))
    n_sample_rows = n_dec * t_dec
    row_s = n_batch * tp
    rows = _round_up(row_s + n_sample_rows, ROW_TILE)

    meta = jnp.broadcast_to(meta_tokens[None], (n_batch, N_META, D_MODEL))
    xp = jnp.concatenate([meta, x_prompt, jnp.zeros((n_batch, tp - t_prompt, D_MODEL), F32)], axis=1)
    x = jnp.concatenate([xp.reshape(row_s, D_MODEL), x_sample.reshape(n_sample_rows, D_MODEL),
                         jnp.zeros((rows - row_s - n_sample_rows, D_MODEL), F32)], axis=0)

    n_pool, page = cache_k.shape[1:3]
    cache_k = jnp.transpose(cache_k, (0, 1, 3, 4, 2)).reshape(depth, n_pool, SB_W, page)
    cache_v = jnp.transpose(cache_v, (0, 1, 3, 4, 2)).reshape(depth, n_pool, SB_W, page)
    c_gate = 2 * M_QK_W + 2 * M_V_W
    zeros_C = jnp.zeros((n_batch, M_HEADS, M_V_DIM, M_QK_DIM), F32)
    zeros_n = jnp.zeros((n_batch, M_HEADS, M_QK_DIM), F32)
    zeros_m = jnp.zeros((n_batch, M_HEADS), F32)
    tail = jnp.zeros((rows - row_s - n_sample_rows, SB_W), F32)

    outs = [[] for _ in range(10)]
    for l in range(depth):
        wm = jnp.concatenate([w_in[l, :, :c_gate], w_in[l, :, c_gate + 2 * M_HEADS:]], axis=1).astype(BF16)
        wg = jnp.pad(w_in[l, :, c_gate:c_gate + 2 * M_HEADS], ((0, 0), (0, GATE_W - 2 * M_HEADS))).astype(BF16)
        bg = jnp.pad(b_if[l], (0, GATE_W - 2 * M_HEADS)).reshape(1, GATE_W)
        qg = jnp.tile(sb_q_gain[l], SB_HEADS).reshape(1, SB_W)
        kg = jnp.tile(sb_k_gain[l], SB_HEADS).reshape(1, SB_W)
        mq, mk, mv, mo, g, sq, sk, sv, ga, gb, sqb, skb, svb = _proj(x, norm1[l].reshape(1, D_MODEL), wm, wg, bg, qg, kg)

        a_p, C_p, n_p, m_p = _mlstm(mq, mk, mv, g, mo, mlstm_gain[l], zeros_C, zeros_n, zeros_m,
                                    row0=0, n_seq=n_batch, nc=tp // M_CHUNK, L=M_CHUNK, t_real=t_prompt)
        a_s, C_s, n_s, m_s = _mlstm(mq, mk, mv, g, mo, mlstm_gain[l], state_C[l], state_n[l], state_m[l],
                                    row0=row_s, n_seq=n_dec, nc=1, L=t_dec, t_real=t_dec)
        bias2 = sb_logit_bias[l] * LOG2E
        hs_p = _sb_prompt(sqb, skb, svb, bias2, n_batch=n_batch, tp=tp)
        hs_s = _sb_decode(sq, sk, sv, cache_k, cache_v, page_table, bias2,
                          layer=l, row0=row_s, t_dec=t_dec)
        a = jnp.concatenate([a_p, a_s, tail], axis=0)
        hs = jnp.concatenate([hs_p, hs_s, tail], axis=0)
        x = _merge(x, a, hs, ga, gb, w_br_a[l].astype(BF16), w_br_b[l].astype(BF16), w_o[l].astype(BF16))
        x = _mlp(x, norm2[l].reshape(1, D_MODEL), w_up[l].astype(BF16), w_down[l].astype(BF16))

        kv_p = lambda t: t[:row_s].reshape(n_batch, tp, SB_HEADS, SB_HEAD_DIM)[:, :t_prompt]
        kv_s = lambda t: t[row_s:row_s + n_sample_rows].reshape(n_dec, t_dec, SB_HEADS, SB_HEAD_DIM)
        for lst, val in zip(outs, (kv_p(sk), kv_p(sv), C_p, n_p, m_p, kv_s(sk), kv_s(sv), C_s, n_s, m_s)):
            lst.append(val)

    y_prompt = x[:row_s].reshape(n_batch, tp, D_MODEL)[:, N_META:t_prompt]
    y_sample = x[row_s:row_s + n_sample_rows].reshape(n_dec, t_dec, D_MODEL)
    return (y_prompt, y_sample) + tuple(jnp.stack(o) for o in outs)
```

```python
import functools

import jax
import jax.numpy as jnp
from jax import lax
from jax.experimental import pallas as pl
from jax.experimental.pallas import tpu as pltpu

F32 = jnp.float32
BF16 = jnp.bfloat16
HIGHEST = lax.Precision.HIGHEST

D_MODEL = 1024
N_META = 16
M_HEADS = 4
M_V_DIM = 128
M_QK_DIM = 64
SB_HEADS = 8
SB_HEAD_DIM = 64
D_FF = 4 * D_MODEL
M_QK_W = M_HEADS * M_QK_DIM
M_V_W = M_HEADS * M_V_DIM
SB_W = SB_HEADS * SB_HEAD_DIM
EPS = 1e-6
NEG = -1e30
LOG2E = 1.4426950408889634
GATE_W = 128
PAIR_W = 2 * SB_HEAD_DIM

ROW_TILE = 256
M_CHUNK = 256
SB_BLOCK = 256
SB_STAGGER = 1
VMEM_LIMIT = 56 * 1024 * 1024

_C_MQ, _C_MK, _C_MV, _C_MO = 0, 256, 512, 1024
_C_SQ, _C_SK, _C_SV, _C_GA, _C_GB, _C_END = 1536, 2048, 2560, 3072, 4096, 5120


def _softplus(z):
    return jnp.maximum(z, 0.0) + jnp.log1p(jnp.exp(-jnp.abs(z)))


def _softplus2(z):
    neg_abs = pltpu.bitcast(pltpu.bitcast(z, jnp.uint32) | jnp.uint32(0x80000000), F32)
    return jnp.maximum(z, 0.0) + jnp.log(1.0 + jnp.exp2(neg_abs)) * LOG2E


def _sigmoid(z):
    return 1.0 / (1.0 + jnp.exp(-z))


def _split_bf16(x):
    hi = x.astype(BF16)
    lo = (x - hi.astype(F32)).astype(BF16)
    return hi, lo


def _dot(a, b):
    return jnp.dot(a, b, preferred_element_type=F32)


def _dot_nt(a, b):
    return lax.dot_general(a, b, (((1,), (1,)), ((), ())), preferred_element_type=F32)


def _dot_tn(a, b):
    return lax.dot_general(a, b, (((0,), (0,)), ((), ())), preferred_element_type=F32)


def _head_norm(y, gain_full):
    y2 = y * y
    head = lax.broadcasted_iota(jnp.int32, (1, SB_W), 1) // SB_HEAD_DIM
    scale = jnp.zeros_like(y)
    for h in range(SB_HEADS):
        m = head == h
        ssq = jnp.sum(jnp.where(m, y2, 0.0), axis=-1, keepdims=True)
        r = lax.rsqrt(ssq * (1.0 / SB_HEAD_DIM) + EPS)
        scale = jnp.where(m, r, scale)
    return y * scale * gain_full


def _proj_kernel(x_ref, n1_ref, wm_ref, wg_ref, bg_ref, qg_ref, kg_ref, fill_ref,
                 mq_ref, mk_ref, mv_ref, mo_ref, g_ref, sq_ref, sk_ref, sv_ref, ga_ref, gb_ref,
                 sqb_ref, skb_ref, svb_ref, kt_ref, vt_ref, *, n_prompt_tiles):
    x = x_ref[...]
    ms = jnp.mean(x * x, axis=-1, keepdims=True)
    h = (x * lax.rsqrt(ms + EPS) * n1_ref[...]).astype(BF16)

    def mm(lo, hi):
        return _dot(h, wm_ref[:, lo:hi])

    mq_ref[...] = mm(_C_MQ, _C_MK)
    mk_ref[...] = mm(_C_MK, _C_MV) * (M_QK_DIM ** -0.5)
    mv_ref[...] = mm(_C_MV, _C_MO)
    mo_ref[...] = _sigmoid(mm(_C_MO, _C_SQ))
    g = _dot(h, wg_ref[...]) + bg_ref[...]
    lane = lax.broadcasted_iota(jnp.int32, (1, GATE_W), 1)
    g_ref[...] = jnp.where(lane < M_HEADS, g, -_softplus(-g))
    sq = _head_norm(mm(_C_SQ, _C_SK), qg_ref[...]) * (SB_HEAD_DIM ** -0.5 * LOG2E)
    sk = _head_norm(mm(_C_SK, _C_SV), kg_ref[...])
    sv = mm(_C_SV, _C_GA)
    sq_ref[...], sk_ref[...], sv_ref[...] = sq, sk, sv

    @pl.when(pl.program_id(0) < n_prompt_tiles)
    def _():
        kt_ref[...] = sk.T.reshape(SB_HEADS, SB_HEAD_DIM, sk.shape[0])
        vt_ref[...] = sv.T.reshape(SB_HEADS, SB_HEAD_DIM, sv.shape[0])

    first = lax.broadcasted_iota(jnp.int32, (1, PAIR_W), 1) < SB_HEAD_DIM
    for src, dst, row in ((sq, sqb_ref, 0), (sk, skb_ref, 1), (sv, svb_ref, 2)):
        for head in range(SB_HEADS):
            grp = slice(head * PAIR_W, (head + 1) * PAIR_W)
            pair = src[:, (head // 2) * PAIR_W:(head // 2 + 1) * PAIR_W]
            own = first if head % 2 == 0 else jnp.logical_not(first)
            dst[:, grp] = jnp.where(own, pair, fill_ref[row:row + 1, grp]).astype(BF16)
    ga_ref[...] = _sigmoid(mm(_C_GA, _C_GB))
    gb_ref[...] = _sigmoid(mm(_C_GB, _C_END))


def _proj(x, n1, wm, wg, bg, qg, kg, fill, *, n_batch, tp, t_prompt):
    rows = x.shape[0]
    tm = ROW_TILE
    nt = tp // tm
    widths = (M_QK_W, M_QK_W, M_V_W, M_V_W, GATE_W, SB_W, SB_W, SB_W, D_MODEL, D_MODEL)
    bf_widths = (SB_HEADS * PAIR_W,) * 3
    full = lambda a: pl.BlockSpec(a.shape, lambda i: (0,) * a.ndim)
    last = n_batch * nt - 1

    def kv_map(i):
        i = jnp.minimum(i, last)
        return (i // nt, 0, 0, i % nt)

    kv_spec = pl.BlockSpec((None, SB_HEADS, SB_HEAD_DIM, tm), kv_map)
    kv_shape = jax.ShapeDtypeStruct((n_batch, SB_HEADS, SB_HEAD_DIM, t_prompt), F32)
    return pl.pallas_call(
        functools.partial(_proj_kernel, n_prompt_tiles=n_batch * nt),
        grid=(rows // tm,),
        in_specs=[pl.BlockSpec((tm, D_MODEL), lambda i: (i, 0)),
                  full(n1), full(wm), full(wg), full(bg), full(qg), full(kg), full(fill)],
        out_specs=[pl.BlockSpec((tm, w), lambda i: (i, 0)) for w in widths + bf_widths] + [kv_spec] * 2,
        out_shape=[jax.ShapeDtypeStruct((rows, w), F32) for w in widths]
        + [jax.ShapeDtypeStruct((rows, w), BF16) for w in bf_widths] + [kv_shape] * 2,
        compiler_params=pltpu.CompilerParams(dimension_semantics=("arbitrary",),
                                             vmem_limit_bytes=VMEM_LIMIT),
        name="proj",
    )(x, n1, wm, wg, bg, qg, kg, fill)


def _mlstm_kernel(q_ref, k_ref, v_ref, g_ref, mo_ref, gain_ref, C0_ref, n0_ref, m0_ref,
                  a_ref, Co_ref, no_ref, mout_ref, C_scr, n_scr, m_scr, *, L, t_real, nc, G):
    c = pl.program_id(1)

    @pl.when(c == 0)
    def _():
        C_scr[...] = C0_ref[...]
        n_scr[...] = n0_ref[...]
        m_scr[...] = m0_ref[...]

    mxu = BF16 if L >= 128 else F32
    row = lax.broadcasted_iota(jnp.int32, (L, 1), 0)
    lane = lax.broadcasted_iota(jnp.int32, (1, GATE_W), 1)
    valid = (c * L + row) < t_real
    ti = lax.broadcasted_iota(jnp.int32, (L, L), 0)
    si = lax.broadcasted_iota(jnp.int32, (L, L), 1)
    causal = si <= ti
    eye = si == ti

    for seq, h in [(seq, h) for seq in range(G) for h in range(M_HEADS)]:
        rs = slice(seq * L, (seq + 1) * L)
        if h == 0:
            g = jnp.where(valid, g_ref[rs, :], jnp.where(lane < M_HEADS, NEG, 0.0))
            b_all = jnp.dot(jnp.where(causal, 1.0, 0.0), g, precision=HIGHEST, preferred_element_type=F32)
        qf = q_ref[rs, h * M_QK_DIM:(h + 1) * M_QK_DIM]
        kf = k_ref[rs, h * M_QK_DIM:(h + 1) * M_QK_DIM]
        vf = v_ref[rs, h * M_V_DIM:(h + 1) * M_V_DIM]
        q, k = qf.astype(mxu), kf.astype(mxu)
        C0 = C_scr[seq, h]
        n0 = n_scr[seq, h:h + 1, :]
        m0 = m_scr[seq, 0:1, h:h + 1]
        logi = g[:, h:h + 1]
        b = b_all[:, M_HEADS + h:M_HEADS + h + 1]
        r_col = logi - b
        r_row = jnp.sum(jnp.where(eye, r_col, 0.0), axis=0, keepdims=True)
        dm = jnp.where(causal, b + r_row, NEG)
        inter = b + m0
        m = jnp.maximum(inter, jnp.max(dm, axis=1, keepdims=True))
        s = _dot_nt(q, k) * jnp.exp(dm - m)
        e_inter = jnp.exp(inter - m)
        num = _dot(s.astype(mxu), vf.astype(mxu)) + e_inter * _dot(q, C0.astype(mxu))
        den = jnp.sum(s, axis=1, keepdims=True) + e_inter * jnp.sum(qf * n0, axis=1, keepdims=True)
        hh = num / jnp.maximum(jnp.abs(den), jnp.exp(-m))
        hn = hh * lax.rsqrt(jnp.mean(hh * hh, axis=-1, keepdims=True) + EPS) * gain_ref[h:h + 1, :]
        a_ref[rs, h * M_V_DIM:(h + 1) * M_V_DIM] = hn * mo_ref[rs, h * M_V_DIM:(h + 1) * M_V_DIM]

        m_new = m[L - 1:L, :]
        b_last = b[L - 1:L, :]
        w = jnp.exp(b_last - b + logi - m_new)
        decay = jnp.exp(b_last + m0 - m_new)
        C_scr[seq, h] = decay * C0 + _dot_tn(k, (vf * w).astype(mxu))
        n_scr[seq, h:h + 1, :] = decay * n0 + jnp.sum(w * kf, axis=0, keepdims=True)
        m_scr[seq, 0:1, h:h + 1] = m_new

    @pl.when(c == nc - 1)
    def _():
        Co_ref[...] = C_scr[...]
        no_ref[...] = n_scr[...]
        mout_ref[...] = m_scr[...]


def _mlstm(mq, mk, mv, g, mo, gain, C0, n0, m0, *, row0, n_seq, nc, L, t_real):
    G = 1
    blk0 = row0 // (G * L)
    rmap = lambda s, c: (blk0 + s * nc + c, 0)
    smap3 = lambda s, c: (s, 0, 0)
    smap4 = lambda s, c: (s, 0, 0, 0)
    m0 = m0.reshape(n_seq, 1, M_HEADS)
    kern = functools.partial(_mlstm_kernel, L=L, t_real=t_real, nc=nc, G=G)
    a, Co, no, mo_out = pl.pallas_call(
        kern,
        grid=(n_seq // G, nc),
        in_specs=[pl.BlockSpec((G * L, M_QK_W), rmap), pl.BlockSpec((G * L, M_QK_W), rmap),
                  pl.BlockSpec((G * L, M_V_W), rmap), pl.BlockSpec((G * L, GATE_W), rmap),
                  pl.BlockSpec((G * L, M_V_W), rmap),
                  pl.BlockSpec((M_HEADS, M_V_DIM), lambda s, c: (0, 0)),
                  pl.BlockSpec((G, M_HEADS, M_QK_DIM, M_V_DIM), smap4),
                  pl.BlockSpec((G, M_HEADS, M_QK_DIM), smap3),
                  pl.BlockSpec((G, 1, M_HEADS), smap3)],
        out_specs=[pl.BlockSpec((G * L, M_V_W), lambda s, c: (s * nc + c, 0)),
                   pl.BlockSpec((G, M_HEADS, M_QK_DIM, M_V_DIM), smap4),
                   pl.BlockSpec((G, M_HEADS, M_QK_DIM), smap3),
                   pl.BlockSpec((G, 1, M_HEADS), smap3)],
        out_shape=[jax.ShapeDtypeStruct((n_seq * nc * L, M_V_W), F32),
                   jax.ShapeDtypeStruct((n_seq, M_HEADS, M_QK_DIM, M_V_DIM), F32),
                   jax.ShapeDtypeStruct((n_seq, M_HEADS, M_QK_DIM), F32),
                   jax.ShapeDtypeStruct((n_seq, 1, M_HEADS), F32)],
        scratch_shapes=[pltpu.VMEM((G, M_HEADS, M_QK_DIM, M_V_DIM), F32),
                        pltpu.VMEM((G, M_HEADS, M_QK_DIM), F32),
                        pltpu.VMEM((G, 1, M_HEADS), F32)],
        compiler_params=pltpu.CompilerParams(dimension_semantics=("arbitrary", "arbitrary"),
                                             vmem_limit_bytes=VMEM_LIMIT),
        name="mlstm_L%d" % L,
    )(mq, mk, mv, g, mo, gain, C0, n0, m0)
    return a, Co, no, mo_out.reshape(n_seq, M_HEADS)


def _sb_kernel(qi_ref, k1_ref, k2_ref, q_ref, ka_ref, va_ref, kb_ref, vb_ref, o_ref,
               acc_ref, carry_ref, *, T):
    p = pl.program_id(1)
    i = qi_ref[p]
    j1 = k1_ref[p]
    j2 = k2_ref[p]

    @pl.when(j1 == i)
    def _():
        acc_ref[...] = jnp.zeros_like(acc_ref)
        carry_ref[...] = jnp.zeros_like(carry_ref)

    def body(diag, two):
        ti = lax.broadcasted_iota(jnp.int32, (T, T), 0)
        si = lax.broadcasted_iota(jnp.int32, (T, T), 1)
        upper = jnp.where(ti > si, 1.0, 0.0).astype(BF16)
        valid = si < ti
        blocks = [(ka_ref, va_ref, diag)] + ([(kb_ref, vb_ref, False)] if two else [])
        units = [(h, blk) for h in range(SB_HEADS) for blk in range(len(blocks))]
        sls = [slice((h // 2) * PAIR_W, (h // 2 + 1) * PAIR_W) for h in range(SB_HEADS)]
        own = [slice(h * PAIR_W, (h + 1) * PAIR_W) for h in range(SB_HEADS)]
        zs, rest, behind = {}, {}, {}
        carries = [carry_ref[h] for h in range(SB_HEADS)]
        outs = [None] * SB_HEADS

        def scores(h, blk):
            zs[h, blk] = _dot_nt(q_ref[:, own[h]], blocks[blk][0][:, own[h]])

        def sums(h, blk):
            z = zs.pop((h, blk))
            sp = _softplus2(z)
            if blocks[blk][2]:
                sp = jnp.where(valid, sp, 0.0)
            behind[h, blk] = _dot(sp.astype(BF16), upper) + jnp.concatenate([carries[h]] * (T // PAIR_W), axis=1)
            rest[h, blk] = z - sp
            carries[h] = carries[h] + jnp.sum(sp, axis=1, keepdims=True)

        def weights(h, blk):
            a = jnp.exp2(rest.pop((h, blk)) - behind.pop((h, blk)))
            if blocks[blk][2]:
                a = jnp.where(valid, a, 0.0)
            o = _dot(a.astype(BF16), blocks[blk][1][:, own[h]])
            outs[h] = o if outs[h] is None else outs[h] + o

        for s in range(len(units) + 2 * SB_STAGGER):
            if s < len(units):
                scores(*units[s])
            if 0 <= s - SB_STAGGER < len(units):
                sums(*units[s - SB_STAGGER])
            if 0 <= s - 2 * SB_STAGGER < len(units):
                weights(*units[s - 2 * SB_STAGGER])
        for h in range(SB_HEADS):
            carry_ref[h] = carries[h]
        for pair in range(SB_HEADS // 2):
            acc_ref[:, sls[2 * pair]] += outs[2 * pair] + outs[2 * pair + 1]

    for diag in (True, False):
        for two in (True, False):
            cond = ((j1 == i) if diag else (j1 < i)) & ((j2 >= 0) if two else (j2 < 0))
            pl.when(cond)(functools.partial(body, diag, two))

    @pl.when((j1 == 0) | (j2 == 0))
    def _():
        o_ref[...] = acc_ref[...]


def _sb_prompt(sq, sk, sv, *, n_batch, tp):
    T = SB_BLOCK
    nq = tp // T
    steps = [(i, j, j - 1 if j >= 1 else -1) for i in range(nq) for j in range(i, -1, -2)]
    qi, k1, k2 = (jnp.asarray(c, jnp.int32) for c in zip(*steps))
    qmap = lambda b, p, qi, k1, k2: (b * nq + qi[p], 0)
    amap = lambda b, p, qi, k1, k2: (b * nq + k1[p], 0)
    bmap = lambda b, p, qi, k1, k2: (b * nq + jnp.maximum(k2[p], 0), 0)
    return pl.pallas_call(
        functools.partial(_sb_kernel, T=T),
        grid_spec=pltpu.PrefetchScalarGridSpec(
            num_scalar_prefetch=3,
            grid=(n_batch, len(steps)),
            in_specs=[pl.BlockSpec((T, SB_HEADS * PAIR_W), qmap),
                      pl.BlockSpec((T, SB_HEADS * PAIR_W), amap), pl.BlockSpec((T, SB_HEADS * PAIR_W), amap),
                      pl.BlockSpec((T, SB_HEADS * PAIR_W), bmap), pl.BlockSpec((T, SB_HEADS * PAIR_W), bmap)],
            out_specs=pl.BlockSpec((T, SB_W), qmap),
            scratch_shapes=[pltpu.VMEM((T, SB_W), F32), pltpu.VMEM((SB_HEADS, T, PAIR_W), F32)]),
        out_shape=jax.ShapeDtypeStruct((n_batch * tp, SB_W), F32),
        compiler_params=pltpu.CompilerParams(dimension_semantics=("arbitrary", "arbitrary"),
                                             vmem_limit_bytes=VMEM_LIMIT),
        name="sb_prompt",
    )(qi, k1, k2, sq, sk, sv, sk, sv)


def _sb_decode_kernel(pt_ref, bias_ref, q_ref, kc_ref, vc_ref, *refs, n_pages, page, t_dec):
    del pt_ref
    k_pages = refs[:n_pages]
    v_pages = refs[n_pages:2 * n_pages]
    o_ref = refs[2 * n_pages]
    nc = SB_HEADS * t_dec
    q = q_ref[...]
    crow = lax.broadcasted_iota(jnp.int32, (nc, SB_W), 0)
    clane = lax.broadcasted_iota(jnp.int32, (nc, SB_W), 1)
    own_head = (crow // t_dec) == (clane // SB_HEAD_DIM)
    q_bd = jnp.where(own_head, jnp.concatenate([q] * SB_HEADS, axis=0), 0.0).astype(BF16)
    bias = bias_ref[...]
    ji = lax.broadcasted_iota(jnp.int32, (page, page), 0)
    si = lax.broadcasted_iota(jnp.int32, (page, page), 1)
    upper = jnp.where(ji > si, 1.0, 0.0).astype(BF16)

    pad = jnp.zeros((page - t_dec, SB_W), F32)
    kc = jnp.concatenate([kc_ref[...], pad], axis=0).astype(BF16)
    vc = jnp.concatenate([vc_ref[...], pad], axis=0).astype(BF16)
    t_idx = lax.broadcasted_iota(jnp.int32, (nc, page), 0) % t_dec
    s_idx = lax.broadcasted_iota(jnp.int32, (nc, page), 1)
    valid = s_idx < t_idx
    order = list(range(n_pages - 1, -1, -1))
    zs = [_dot_nt(q_bd, kc) + bias] + [_dot(q_bd, k_pages[pg][...].astype(BF16)) + bias for pg in order]
    rests, behinds = [], []
    carry = jnp.zeros((nc, 1), F32)
    for n, z in enumerate(zs):
        sp = _softplus2(z)
        if n == 0:
            sp = jnp.where(valid, sp, 0.0)
        behinds.append(_dot(sp.astype(BF16), upper) + carry)
        rests.append(z - sp)
        carry = carry + jnp.sum(sp, axis=1, keepdims=True)
    acc = jnp.zeros((nc, SB_W), F32)
    for n, (rest, behind) in enumerate(zip(rests, behinds)):
        a = jnp.exp2(rest - behind)
        if n == 0:
            a = jnp.where(valid, a, 0.0).astype(BF16)
            acc = acc + _dot(a, vc)
        else:
            acc = acc + _dot_nt(a.astype(BF16), v_pages[order[n - 1]][...].astype(BF16))

    olane = lax.broadcasted_iota(jnp.int32, (t_dec, SB_W), 1) // SB_HEAD_DIM
    out = jnp.zeros((t_dec, SB_W), F32)
    for h in range(SB_HEADS):
        out = out + jnp.where(olane == h, acc[h * t_dec:(h + 1) * t_dec, :], 0.0)
    o_ref[...] = out


def _sb_decode(sq, sk, sv, cache_k, cache_v, page_table, bias, *, layer, row0, t_dec):
    n_seq, n_pages = page_table.shape
    page = cache_k.shape[3]
    blk0 = row0 // t_dec
    nc = SB_HEADS * t_dec
    bias_row = jnp.repeat(bias, t_dec).reshape(nc, 1)
    pt = page_table.reshape(-1)
    rmap = lambda n, pt: (blk0 + n, 0)

    def pmap(pg):
        return lambda n, pt: (layer, pt[n * n_pages + pg], 0, 0)

    page_specs = [pl.BlockSpec((None, None, SB_W, page), pmap(pg)) for pg in range(n_pages)]
    kern = functools.partial(_sb_decode_kernel, n_pages=n_pages, page=page, t_dec=t_dec)
    return pl.pallas_call(
        kern,
        grid_spec=pltpu.PrefetchScalarGridSpec(
            num_scalar_prefetch=1,
            grid=(n_seq,),
            in_specs=[pl.BlockSpec((nc, 1), lambda n, pt: (0, 0)),
                      pl.BlockSpec((t_dec, SB_W), rmap), pl.BlockSpec((t_dec, SB_W), rmap),
                      pl.BlockSpec((t_dec, SB_W), rmap)] + page_specs + page_specs,
            out_specs=pl.BlockSpec((t_dec, SB_W), lambda n, pt: (n, 0))),
        out_shape=jax.ShapeDtypeStruct((n_seq * t_dec, SB_W), F32),
        compiler_params=pltpu.CompilerParams(dimension_semantics=("arbitrary",),
                                             vmem_limit_bytes=VMEM_LIMIT),
        name="sb_decode",
    )(pt, bias_row, sq, sk, sv, *([cache_k] * n_pages), *([cache_v] * n_pages))


def _merge_kernel(x_ref, a_ref, hs_ref, ga_ref, gb_ref, wa_ref, wb_ref, wo_ref, o_ref):
    ya = _dot(a_ref[...].astype(BF16), wa_ref[...])
    yb = _dot(hs_ref[...].astype(BF16), wb_ref[...])
    mix = ga_ref[...] * ya + gb_ref[...] * yb
    o_ref[...] = x_ref[...] + _dot(mix.astype(BF16), wo_ref[...])


def _merge(x, a, hs, ga, gb, wa, wb, wo):
    rows = x.shape[0]
    tm = ROW_TILE
    rspec = lambda w: pl.BlockSpec((tm, w), lambda i: (i, 0))
    full = lambda a: pl.BlockSpec(a.shape, lambda i: (0,) * a.ndim)
    return pl.pallas_call(
        _merge_kernel,
        grid=(rows // tm,),
        in_specs=[rspec(D_MODEL), rspec(M_V_W), rspec(SB_W), rspec(D_MODEL), rspec(D_MODEL),
                  full(wa), full(wb), full(wo)],
        out_specs=rspec(D_MODEL),
        out_shape=jax.ShapeDtypeStruct((rows, D_MODEL), F32),
        compiler_params=pltpu.CompilerParams(dimension_semantics=("arbitrary",),
                                             vmem_limit_bytes=VMEM_LIMIT),
        name="merge",
    )(x, a, hs, ga, gb, wa, wb, wo)


_FF_CHUNK = 1024


def _mlp_kernel(x_ref, n2_ref, wu_ref, wd_ref, o_ref):
    x = x_ref[...]
    ms = jnp.mean(x * x, axis=-1, keepdims=True)
    h = (x * lax.rsqrt(ms + EPS) * n2_ref[...]).astype(BF16)
    acc = x
    for c in range(D_FF // _FF_CHUNK):
        u = jnp.maximum(_dot(h, wu_ref[:, c * _FF_CHUNK:(c + 1) * _FF_CHUNK]), 0.0)
        acc = acc + _dot((u * u).astype(BF16), wd_ref[c * _FF_CHUNK:(c + 1) * _FF_CHUNK, :])
    o_ref[...] = acc


def _mlp(x, n2, wu, wd):
    rows = x.shape[0]
    tm = ROW_TILE
    full = lambda a: pl.BlockSpec(a.shape, lambda i: (0,) * a.ndim)
    return pl.pallas_call(
        _mlp_kernel,
        grid=(rows // tm,),
        in_specs=[pl.BlockSpec((tm, D_MODEL), lambda i: (i, 0)), full(n2), full(wu), full(wd)],
        out_specs=pl.BlockSpec((tm, D_MODEL), lambda i: (i, 0)),
        out_shape=jax.ShapeDtypeStruct((rows, D_MODEL), F32),
        compiler_params=pltpu.CompilerParams(dimension_semantics=("arbitrary",),
                                             vmem_limit_bytes=VMEM_LIMIT),
        name="mlp",
    )(x, n2, wu, wd)


def _round_up(n, m):
    return (n + m - 1) // m * m


def kernel(x_prompt, x_sample, cache_k, cache_v, state_C, state_n, state_m, page_table, meta_tokens,
           norm1, w_in, b_if, sb_q_gain, sb_k_gain, sb_logit_bias, mlstm_gain, w_br_a, w_br_b, w_o,
           norm2, w_up, w_down):
    n_batch, seq = x_prompt.shape[:2]
    n_dec, t_dec = x_sample.shape[:2]
    depth = w_in.shape[0]
    t_prompt = N_META + seq
    tp = _round_up(t_prompt, max(SB_BLOCK, M_CHUNK, ROW_TILE))
    n_sample_rows = n_dec * t_dec
    row_s = n_batch * tp
    rows = _round_up(row_s + n_sample_rows, ROW_TILE)

    meta = jnp.broadcast_to(meta_tokens[None], (n_batch, N_META, D_MODEL))
    xp = jnp.concatenate([meta, x_prompt, jnp.zeros((n_batch, tp - t_prompt, D_MODEL), F32)], axis=1)
    x = jnp.concatenate([xp.reshape(row_s, D_MODEL), x_sample.reshape(n_sample_rows, D_MODEL),
                         jnp.zeros((rows - row_s - n_sample_rows, D_MODEL), F32)], axis=0)

    n_pool, page = cache_k.shape[1:3]
    cache_k = jnp.transpose(cache_k, (0, 1, 3, 4, 2)).reshape(depth, n_pool, SB_W, page)
    cache_v = jnp.transpose(cache_v, (0, 1, 3, 4, 2)).reshape(depth, n_pool, SB_W, page)
    c_gate = 2 * M_QK_W + 2 * M_V_W
    zeros_C = jnp.zeros((n_batch, M_HEADS, M_QK_DIM, M_V_DIM), F32)
    state_Ct = jnp.swapaxes(state_C, -1, -2)
    zeros_n = jnp.zeros((n_batch, M_HEADS, M_QK_DIM), F32)
    zeros_m = jnp.zeros((n_batch, M_HEADS), F32)
    tail = jnp.zeros((rows - row_s - n_sample_rows, SB_W), F32)

    fill_pos = jnp.asarray([h * PAIR_W + (SB_HEAD_DIM if h % 2 == 0 else 0) for h in range(SB_HEADS)])
    outs = [[] for _ in range(10)]
    for l in range(depth):
        wm = jnp.concatenate([w_in[l, :, :c_gate], w_in[l, :, c_gate + 2 * M_HEADS:]], axis=1).astype(BF16)
        wg = jnp.pad(w_in[l, :, c_gate:c_gate + 2 * M_HEADS], ((0, 0), (0, GATE_W - 2 * M_HEADS))).astype(BF16)
        bg = jnp.pad(b_if[l], (0, GATE_W - 2 * M_HEADS)).reshape(1, GATE_W)
        qg = jnp.tile(sb_q_gain[l], SB_HEADS).reshape(1, SB_W)
        kg = jnp.tile(sb_k_gain[l], SB_HEADS).reshape(1, SB_W)
        bias2 = sb_logit_bias[l] * LOG2E
        bias_hi = bias2.astype(BF16).astype(F32)
        zero_row = jnp.zeros((SB_HEADS * PAIR_W,), F32)
        fill = jnp.stack([zero_row.at[fill_pos].set(1.0).at[fill_pos + 1].set(1.0),
                          zero_row.at[fill_pos].set(bias_hi).at[fill_pos + 1].set(bias2 - bias_hi),
                          zero_row])
        mq, mk, mv, mo, g, sq, sk, sv, ga, gb, sqb, skb, svb, kt, vt = _proj(
            x, norm1[l].reshape(1, D_MODEL), wm, wg, bg, qg, kg, fill,
            n_batch=n_batch, tp=tp, t_prompt=t_prompt)

        a_p, C_p, n_p, m_p = _mlstm(mq, mk, mv, g, mo, mlstm_gain[l], zeros_C, zeros_n, zeros_m,
                                    row0=0, n_seq=n_batch, nc=tp // M_CHUNK, L=M_CHUNK, t_real=t_prompt)
        a_s, C_s, n_s, m_s = _mlstm(mq, mk, mv, g, mo, mlstm_gain[l], state_Ct[l], state_n[l], state_m[l],
                                    row0=row_s, n_seq=n_dec, nc=1, L=t_dec, t_real=t_dec)
        hs_p = _sb_prompt(sqb, skb, svb, n_batch=n_batch, tp=tp)
        hs_s = _sb_decode(sq, sk, sv, cache_k, cache_v, page_table, bias2,
                          layer=l, row0=row_s, t_dec=t_dec)
        a = jnp.concatenate([a_p, a_s, tail], axis=0)
        hs = jnp.concatenate([hs_p, hs_s, tail], axis=0)
        x = _merge(x, a, hs, ga, gb, w_br_a[l].astype(BF16), w_br_b[l].astype(BF16), w_o[l].astype(BF16))
        x = _mlp(x, norm2[l].reshape(1, D_MODEL), w_up[l].astype(BF16), w_down[l].astype(BF16))

        kv_s = lambda t: t[row_s:row_s + n_sample_rows].reshape(n_dec, t_dec, SB_HEADS, SB_HEAD_DIM)
        for lst, val in zip(outs, (kt, vt, C_p, n_p, m_p, kv_s(sk), kv_s(sv), C_s, n_s, m_s)):
            lst.append(val)

    y_prompt = x[:row_s].reshape(n_batch, tp, D_MODEL)[:, N_META:t_prompt]
    y_sample = x[row_s:row_s + n_sample_rows].reshape(n_dec, t_dec, D_MODEL)
    outs = [jnp.stack(o) for o in outs]
    outs[0], outs[1] = (jnp.transpose(t, (0, 1, 4, 2, 3)) for t in outs[:2])
    outs[2], outs[7] = jnp.swapaxes(outs[2], -1, -2), jnp.swapaxes(outs[7], -1, -2)
    return (y_prompt, y_sample) + tuple(outs)
```

```python
import functools

import jax
import jax.numpy as jnp
from jax import lax
from jax.experimental import pallas as pl
from jax.experimental.pallas import tpu as pltpu

F32 = jnp.float32
BF16 = jnp.bfloat16
HIGHEST = lax.Precision.HIGHEST

D_MODEL = 1024
N_META = 16
M_HEADS = 4
M_V_DIM = 128
M_QK_DIM = 64
SB_HEADS = 8
SB_HEAD_DIM = 64
D_FF = 4 * D_MODEL
M_QK_W = M_HEADS * M_QK_DIM
M_V_W = M_HEADS * M_V_DIM
SB_W = SB_HEADS * SB_HEAD_DIM
EPS = 1e-6
NEG = -1e30
LOG2E = 1.4426950408889634
GATE_W = 128
PAIR_W = 2 * SB_HEAD_DIM

ROW_TILE = 256
M_CHUNK = 256
SB_BLOCK = 256
SB_STAGGER = 1
VMEM_LIMIT = 56 * 1024 * 1024

_C_MQ, _C_MK, _C_MV, _C_MO = 0, 256, 512, 1024
_C_SQ, _C_SK, _C_SV, _C_GA, _C_GB, _C_END = 1536, 2048, 2560, 3072, 4096, 5120


def _softplus(z):
    return jnp.maximum(z, 0.0) + jnp.log1p(jnp.exp(-jnp.abs(z)))


def _softplus2(z):
    neg_abs = pltpu.bitcast(pltpu.bitcast(z, jnp.uint32) | jnp.uint32(0x80000000), F32)
    return jnp.maximum(z, 0.0) + jnp.log(1.0 + jnp.exp2(neg_abs)) * LOG2E


def _sigmoid(z):
    return 1.0 / (1.0 + jnp.exp(-z))


def _split_bf16(x):
    hi = x.astype(BF16)
    lo = (x - hi.astype(F32)).astype(BF16)
    return hi, lo


def _dot(a, b):
    return jnp.dot(a, b, preferred_element_type=F32)


def _dot_nt(a, b):
    return lax.dot_general(a, b, (((1,), (1,)), ((), ())), preferred_element_type=F32)


def _dot_tn(a, b):
    return lax.dot_general(a, b, (((0,), (0,)), ((), ())), preferred_element_type=F32)


def _head_norm(y, gain_full):
    y2 = y * y
    head = lax.broadcasted_iota(jnp.int32, (1, SB_W), 1) // SB_HEAD_DIM
    scale = jnp.zeros_like(y)
    for h in range(SB_HEADS):
        m = head == h
        ssq = jnp.sum(jnp.where(m, y2, 0.0), axis=-1, keepdims=True)
        r = lax.rsqrt(ssq * (1.0 / SB_HEAD_DIM) + EPS)
        scale = jnp.where(m, r, scale)
    return y * scale * gain_full


def _proj_kernel(x_ref, n1_ref, wm_ref, wg_ref, bg_ref, qg_ref, kg_ref, fill_ref,
                 mq_ref, mk_ref, mv_ref, mo_ref, g_ref, sq_ref, sk_ref, sv_ref, ga_ref, gb_ref,
                 sqb_ref, skb_ref, svb_ref, kt_ref, vt_ref, *, n_prompt_tiles):
    x = x_ref[...]
    ms = jnp.mean(x * x, axis=-1, keepdims=True)
    h = (x * lax.rsqrt(ms + EPS) * n1_ref[...]).astype(BF16)

    def mm(lo, hi):
        return _dot(h, wm_ref[:, lo:hi])

    mq_ref[...] = mm(_C_MQ, _C_MK)
    mk_ref[...] = mm(_C_MK, _C_MV) * (M_QK_DIM ** -0.5)
    mv_ref[...] = mm(_C_MV, _C_MO)
    mo_ref[...] = _sigmoid(mm(_C_MO, _C_SQ))
    g = _dot(h, wg_ref[...]) + bg_ref[...]
    lane = lax.broadcasted_iota(jnp.int32, (1, GATE_W), 1)
    g_ref[...] = jnp.where(lane < M_HEADS, g, -_softplus(-g))
    sq = _head_norm(mm(_C_SQ, _C_SK), qg_ref[...]) * (SB_HEAD_DIM ** -0.5 * LOG2E)
    sk = _head_norm(mm(_C_SK, _C_SV), kg_ref[...])
    sv = mm(_C_SV, _C_GA)
    sq_ref[...], sk_ref[...], sv_ref[...] = sq, sk, sv

    @pl.when(pl.program_id(0) < n_prompt_tiles)
    def _():
        kt_ref[...] = sk.T.reshape(SB_HEADS, SB_HEAD_DIM, sk.shape[0])
        vt_ref[...] = sv.T.reshape(SB_HEADS, SB_HEAD_DIM, sv.shape[0])

    first = lax.broadcasted_iota(jnp.int32, (1, PAIR_W), 1) < SB_HEAD_DIM
    for src, dst, row in ((sq, sqb_ref, 0), (sk, skb_ref, 1), (sv, svb_ref, 2)):
        for head in range(SB_HEADS):
            grp = slice(head * PAIR_W, (head + 1) * PAIR_W)
            pair = src[:, (head // 2) * PAIR_W:(head // 2 + 1) * PAIR_W]
            own = first if head % 2 == 0 else jnp.logical_not(first)
            dst[:, grp] = jnp.where(own, pair, fill_ref[row:row + 1, grp]).astype(BF16)
    ga_ref[...] = _sigmoid(mm(_C_GA, _C_GB))
    gb_ref[...] = _sigmoid(mm(_C_GB, _C_END))


def _proj(x, n1, wm, wg, bg, qg, kg, fill, *, n_batch, tp, t_prompt):
    rows = x.shape[0]
    tm = ROW_TILE
    nt = tp // tm
    widths = (M_QK_W, M_QK_W, M_V_W, M_V_W, GATE_W, SB_W, SB_W, SB_W, D_MODEL, D_MODEL)
    bf_widths = (SB_HEADS * PAIR_W,) * 3
    full = lambda a: pl.BlockSpec(a.shape, lambda i: (0,) * a.ndim)
    last = n_batch * nt - 1

    def kv_map(i):
        i = jnp.minimum(i, last)
        return (i // nt, 0, 0, i % nt)

    kv_spec = pl.BlockSpec((None, SB_HEADS, SB_HEAD_DIM, tm), kv_map)
    kv_shape = jax.ShapeDtypeStruct((n_batch, SB_HEADS, SB_HEAD_DIM, t_prompt), F32)
    return pl.pallas_call(
        functools.partial(_proj_kernel, n_prompt_tiles=n_batch * nt),
        grid=(rows // tm,),
        in_specs=[pl.BlockSpec((tm, D_MODEL), lambda i: (i, 0)),
                  full(n1), full(wm), full(wg), full(bg), full(qg), full(kg), full(fill)],
        out_specs=[pl.BlockSpec((tm, w), lambda i: (i, 0)) for w in widths + bf_widths] + [kv_spec] * 2,
        out_shape=[jax.ShapeDtypeStruct((rows, w), F32) for w in widths]
        + [jax.ShapeDtypeStruct((rows, w), BF16) for w in bf_widths] + [kv_shape] * 2,
        compiler_params=pltpu.CompilerParams(dimension_semantics=("arbitrary",),
                                             vmem_limit_bytes=VMEM_LIMIT),
        name="proj",
    )(x, n1, wm, wg, bg, qg, kg, fill)


def _mlstm_kernel(q_ref, k_ref, v_ref, g_ref, mo_ref, gain_ref, C0_ref, n0_ref, m0_ref,
                  a_ref, Co_ref, no_ref, mout_ref, C_scr, n_scr, m_scr, *, L, t_real, nc, G):
    c = pl.program_id(1)

    @pl.when(c == 0)
    def _():
        C_scr[...] = C0_ref[...]
        n_scr[...] = n0_ref[...]
        m_scr[...] = m0_ref[...]

    mxu = BF16 if L >= 128 else F32
    row = lax.broadcasted_iota(jnp.int32, (L, 1), 0)
    lane = lax.broadcasted_iota(jnp.int32, (1, GATE_W), 1)
    valid = (c * L + row) < t_real
    ti = lax.broadcasted_iota(jnp.int32, (L, L), 0)
    si = lax.broadcasted_iota(jnp.int32, (L, L), 1)
    causal = si <= ti
    eye = si == ti

    units = [(seq, h) for seq in range(G) for h in range(M_HEADS)]
    gates, live = {}, {}

    def weights_and_scores(seq, h):
        rs = slice(seq * L, (seq + 1) * L)
        if seq not in gates:
            g = jnp.where(valid, g_ref[rs, :], jnp.where(lane < M_HEADS, NEG, 0.0))
            gates[seq] = g, jnp.dot(jnp.where(causal, 1.0, 0.0), g, precision=HIGHEST,
                                    preferred_element_type=F32)
        g, b_all = gates[seq]
        qf = q_ref[rs, h * M_QK_DIM:(h + 1) * M_QK_DIM]
        kf = k_ref[rs, h * M_QK_DIM:(h + 1) * M_QK_DIM]
        q, k = qf.astype(mxu), kf.astype(mxu)
        m0 = m_scr[seq, 0:1, h:h + 1]
        logi = g[:, h:h + 1]
        b = b_all[:, M_HEADS + h:M_HEADS + h + 1]
        r_col = logi - b
        r_row = jnp.sum(jnp.where(eye, r_col, 0.0), axis=0, keepdims=True)
        dm = jnp.where(causal, b + r_row, NEG)
        inter = b + m0
        m = jnp.maximum(inter, jnp.max(dm, axis=1, keepdims=True))
        s = _dot_nt(q, k) * jnp.exp(dm - m)
        live[seq, h] = dict(rs=rs, qf=qf, kf=kf, q=q, k=k, m0=m0, logi=logi, b=b, m=m, s=s,
                            e_inter=jnp.exp(inter - m))

    def outputs(seq, h):
        d = live[seq, h]
        rs, s, m, e_inter = d["rs"], d["s"], d["m"], d["e_inter"]
        vf = v_ref[rs, h * M_V_DIM:(h + 1) * M_V_DIM]
        C0 = C_scr[seq, h]
        n0 = n_scr[seq, h:h + 1, :]
        num = _dot(s.astype(mxu), vf.astype(mxu)) + e_inter * _dot(d["q"], C0.astype(mxu))
        den = jnp.sum(s, axis=1, keepdims=True) + e_inter * jnp.sum(d["qf"] * n0, axis=1, keepdims=True)
        hh = num / jnp.maximum(jnp.abs(den), jnp.exp(-m))
        hn = hh * lax.rsqrt(jnp.mean(hh * hh, axis=-1, keepdims=True) + EPS) * gain_ref[h:h + 1, :]
        a_ref[rs, h * M_V_DIM:(h + 1) * M_V_DIM] = hn * mo_ref[rs, h * M_V_DIM:(h + 1) * M_V_DIM]
        d.update(vf=vf, C0=C0, n0=n0)

    def state_update(seq, h):
        d = live.pop((seq, h))
        m, b, m0 = d["m"], d["b"], d["m0"]
        m_new = m[L - 1:L, :]
        b_last = b[L - 1:L, :]
        w = jnp.exp(b_last - b + d["logi"] - m_new)
        decay = jnp.exp(b_last + m0 - m_new)
        C_scr[seq, h] = decay * d["C0"] + _dot_tn(d["k"], (d["vf"] * w).astype(mxu))
        n_scr[seq, h:h + 1, :] = decay * d["n0"] + jnp.sum(w * d["kf"], axis=0, keepdims=True)
        m_scr[seq, 0:1, h:h + 1] = m_new

    lag = 1 if L < 128 else 0
    for step in range(len(units) + 2 * lag):
        if step < len(units):
            weights_and_scores(*units[step])
        if 0 <= step - lag < len(units):
            outputs(*units[step - lag])
        if 0 <= step - 2 * lag < len(units):
            state_update(*units[step - 2 * lag])

    @pl.when(c == nc - 1)
    def _():
        Co_ref[...] = C_scr[...]
        no_ref[...] = n_scr[...]
        mout_ref[...] = m_scr[...]


def _mlstm(mq, mk, mv, g, mo, gain, C0, n0, m0, *, row0, n_seq, nc, L, t_real):
    G = 1 if nc > 1 else max(d for d in (1, 2, 4, 8) if n_seq % d == 0)
    blk0 = row0 // (G * L)
    rmap = lambda s, c: (blk0 + s * nc + c, 0)
    smap3 = lambda s, c: (s, 0, 0)
    smap4 = lambda s, c: (s, 0, 0, 0)
    m0 = m0.reshape(n_seq, 1, M_HEADS)
    kern = functools.partial(_mlstm_kernel, L=L, t_real=t_real, nc=nc, G=G)
    a, Co, no, mo_out = pl.pallas_call(
        kern,
        grid=(n_seq // G, nc),
        in_specs=[pl.BlockSpec((G * L, M_QK_W), rmap), pl.BlockSpec((G * L, M_QK_W), rmap),
                  pl.BlockSpec((G * L, M_V_W), rmap), pl.BlockSpec((G * L, GATE_W), rmap),
                  pl.BlockSpec((G * L, M_V_W), rmap),
                  pl.BlockSpec((M_HEADS, M_V_DIM), lambda s, c: (0, 0)),
                  pl.BlockSpec((G, M_HEADS, M_QK_DIM, M_V_DIM), smap4),
                  pl.BlockSpec((G, M_HEADS, M_QK_DIM), smap3),
                  pl.BlockSpec((G, 1, M_HEADS), smap3)],
        out_specs=[pl.BlockSpec((G * L, M_V_W), lambda s, c: (s * nc + c, 0)),
                   pl.BlockSpec((G, M_HEADS, M_QK_DIM, M_V_DIM), smap4),
                   pl.BlockSpec((G, M_HEADS, M_QK_DIM), smap3),
                   pl.BlockSpec((G, 1, M_HEADS), smap3)],
        out_shape=[jax.ShapeDtypeStruct((n_seq * nc * L, M_V_W), F32),
                   jax.ShapeDtypeStruct((n_seq, M_HEADS, M_QK_DIM, M_V_DIM), F32),
                   jax.ShapeDtypeStruct((n_seq, M_HEADS, M_QK_DIM), F32),
                   jax.ShapeDtypeStruct((n_seq, 1, M_HEADS), F32)],
        scratch_shapes=[pltpu.VMEM((G, M_HEADS, M_QK_DIM, M_V_DIM), F32),
                        pltpu.VMEM((G, M_HEADS, M_QK_DIM), F32),
                        pltpu.VMEM((G, 1, M_HEADS), F32)],
        compiler_params=pltpu.CompilerParams(dimension_semantics=("arbitrary", "arbitrary"),
                                             vmem_limit_bytes=VMEM_LIMIT),
        name="mlstm_L%d" % L,
    )(mq, mk, mv, g, mo, gain, C0, n0, m0)
    return a, Co, no, mo_out.reshape(n_seq, M_HEADS)


def _sb_kernel(qi_ref, k1_ref, k2_ref, q_ref, ka_ref, va_ref, kb_ref, vb_ref, o_ref,
               acc_ref, carry_ref, *, T):
    p = pl.program_id(1)
    i = qi_ref[p]
    j1 = k1_ref[p]
    j2 = k2_ref[p]

    @pl.when(j1 == i)
    def _():
        acc_ref[...] = jnp.zeros_like(acc_ref)
        carry_ref[...] = jnp.zeros_like(carry_ref)

    def body(diag, two):
        ti = lax.broadcasted_iota(jnp.int32, (T, T), 0)
        si = lax.broadcasted_iota(jnp.int32, (T, T), 1)
        upper = jnp.where(ti > si, 1.0, 0.0).astype(BF16)
        valid = si < ti
        blocks = [(ka_ref, va_ref, diag)] + ([(kb_ref, vb_ref, False)] if two else [])
        units = [(h, blk) for blk in range(len(blocks)) for h in range(SB_HEADS)]
        sls = [slice((h // 2) * PAIR_W, (h // 2 + 1) * PAIR_W) for h in range(SB_HEADS)]
        own = [slice(h * PAIR_W, (h + 1) * PAIR_W) for h in range(SB_HEADS)]
        zs, rest, behind = {}, {}, {}
        carries = [carry_ref[h] for h in range(SB_HEADS)]
        outs = [None] * SB_HEADS

        def scores(h, blk):
            zs[h, blk] = _dot_nt(q_ref[:, own[h]], blocks[blk][0][:, own[h]])

        def sums(h, blk):
            z = zs.pop((h, blk))
            sp = _softplus2(z)
            if blocks[blk][2]:
                sp = jnp.where(valid, sp, 0.0)
            behind[h, blk] = _dot(sp.astype(BF16), upper) + jnp.concatenate([carries[h]] * (T // PAIR_W), axis=1)
            rest[h, blk] = z - sp
            carries[h] = carries[h] + jnp.sum(sp, axis=1, keepdims=True)

        def weights(h, blk):
            a = jnp.exp2(rest.pop((h, blk)) - behind.pop((h, blk)))
            if blocks[blk][2]:
                a = jnp.where(valid, a, 0.0)
            o = _dot(a.astype(BF16), blocks[blk][1][:, own[h]])
            outs[h] = o if outs[h] is None else outs[h] + o

        for s in range(len(units) + 2 * SB_STAGGER):
            if s < len(units):
                scores(*units[s])
            if 0 <= s - SB_STAGGER < len(units):
                sums(*units[s - SB_STAGGER])
            if 0 <= s - 2 * SB_STAGGER < len(units):
                weights(*units[s - 2 * SB_STAGGER])
        for h in range(SB_HEADS):
            carry_ref[h] = carries[h]
        for pair in range(SB_HEADS // 2):
            acc_ref[:, sls[2 * pair]] += outs[2 * pair] + outs[2 * pair + 1]

    for diag in (True, False):
        for two in (True, False):
            cond = ((j1 == i) if diag else (j1 < i)) & ((j2 >= 0) if two else (j2 < 0))
            pl.when(cond)(functools.partial(body, diag, two))

    @pl.when((j1 == 0) | (j2 == 0))
    def _():
        o_ref[...] = acc_ref[...]


def _sb_prompt(sq, sk, sv, *, n_batch, tp):
    T = SB_BLOCK
    nq = tp // T
    steps = [(i, j, j - 1 if j >= 1 else -1) for i in range(nq) for j in range(i, -1, -2)]
    qi, k1, k2 = (jnp.asarray(c, jnp.int32) for c in zip(*steps))
    qmap = lambda b, p, qi, k1, k2: (b * nq + qi[p], 0)
    amap = lambda b, p, qi, k1, k2: (b * nq + k1[p], 0)
    bmap = lambda b, p, qi, k1, k2: (b * nq + jnp.maximum(k2[p], 0), 0)
    return pl.pallas_call(
        functools.partial(_sb_kernel, T=T),
        grid_spec=pltpu.PrefetchScalarGridSpec(
            num_scalar_prefetch=3,
            grid=(n_batch, len(steps)),
            in_specs=[pl.BlockSpec((T, SB_HEADS * PAIR_W), qmap),
                      pl.BlockSpec((T, SB_HEADS * PAIR_W), amap), pl.BlockSpec((T, SB_HEADS * PAIR_W), amap),
                      pl.BlockSpec((T, SB_HEADS * PAIR_W), bmap), pl.BlockSpec((T, SB_HEADS * PAIR_W), bmap)],
            out_specs=pl.BlockSpec((T, SB_W), qmap),
            scratch_shapes=[pltpu.VMEM((T, SB_W), F32), pltpu.VMEM((SB_HEADS, T, PAIR_W), F32)]),
        out_shape=jax.ShapeDtypeStruct((n_batch * tp, SB_W), F32),
        compiler_params=pltpu.CompilerParams(dimension_semantics=("arbitrary", "arbitrary"),
                                             vmem_limit_bytes=VMEM_LIMIT),
        name="sb_prompt",
    )(qi, k1, k2, sq, sk, sv, sk, sv)


def _sb_decode_kernel(pt_ref, bias_ref, q_ref, kc_ref, vc_ref, *refs, n_pages, page, t_dec):
    del pt_ref
    k_pages = refs[:n_pages]
    v_pages = refs[n_pages:2 * n_pages]
    o_ref = refs[2 * n_pages]
    nc = SB_HEADS * t_dec
    q = q_ref[...]
    crow = lax.broadcasted_iota(jnp.int32, (nc, SB_W), 0)
    clane = lax.broadcasted_iota(jnp.int32, (nc, SB_W), 1)
    own_head = (crow // t_dec) == (clane // SB_HEAD_DIM)
    q_bd = jnp.where(own_head, jnp.concatenate([q] * SB_HEADS, axis=0), 0.0).astype(BF16)
    bias = bias_ref[...]
    ji = lax.broadcasted_iota(jnp.int32, (page, page), 0)
    si = lax.broadcasted_iota(jnp.int32, (page, page), 1)
    upper = jnp.where(ji > si, 1.0, 0.0).astype(BF16)

    pad = jnp.zeros((page - t_dec, SB_W), F32)
    kc = jnp.concatenate([kc_ref[...], pad], axis=0).astype(BF16)
    vc = jnp.concatenate([vc_ref[...], pad], axis=0).astype(BF16)
    t_idx = lax.broadcasted_iota(jnp.int32, (nc, page), 0) % t_dec
    s_idx = lax.broadcasted_iota(jnp.int32, (nc, page), 1)
    valid = s_idx < t_idx
    order = list(range(n_pages - 1, -1, -1))
    zs = [_dot_nt(q_bd, kc) + bias] + [_dot(q_bd, k_pages[pg][...].astype(BF16)) + bias for pg in order]
    rests, behinds = [], []
    carry = jnp.zeros((nc, 1), F32)
    for n, z in enumerate(zs):
        sp = _softplus2(z)
        if n == 0:
            sp = jnp.where(valid, sp, 0.0)
        behinds.append(_dot(sp.astype(BF16), upper) + carry)
        rests.append(z - sp)
        carry = carry + jnp.sum(sp, axis=1, keepdims=True)
    acc = jnp.zeros((nc, SB_W), F32)
    for n, (rest, behind) in enumerate(zip(rests, behinds)):
        a = jnp.exp2(rest - behind)
        if n == 0:
            a = jnp.where(valid, a, 0.0).astype(BF16)
            acc = acc + _dot(a, vc)
        else:
            acc = acc + _dot_nt(a.astype(BF16), v_pages[order[n - 1]][...].astype(BF16))

    olane = lax.broadcasted_iota(jnp.int32, (t_dec, SB_W), 1) // SB_HEAD_DIM
    out = jnp.zeros((t_dec, SB_W), F32)
    for h in range(SB_HEADS):
        out = out + jnp.where(olane == h, acc[h * t_dec:(h + 1) * t_dec, :], 0.0)
    o_ref[...] = out


def _sb_decode(sq, sk, sv, cache_k, cache_v, page_table, bias, *, layer, row0, t_dec):
    n_seq, n_pages = page_table.shape
    page = cache_k.shape[3]
    blk0 = row0 // t_dec
    nc = SB_HEADS * t_dec
    bias_row = jnp.repeat(bias, t_dec).reshape(nc, 1)
    pt = page_table.reshape(-1)
    rmap = lambda n, pt: (blk0 + n, 0)

    def pmap(pg):
        return lambda n, pt: (layer, pt[n * n_pages + pg], 0, 0)

    page_specs = [pl.BlockSpec((None, None, SB_W, page), pmap(pg)) for pg in range(n_pages)]
    kern = functools.partial(_sb_decode_kernel, n_pages=n_pages, page=page, t_dec=t_dec)
    return pl.pallas_call(
        kern,
        grid_spec=pltpu.PrefetchScalarGridSpec(
            num_scalar_prefetch=1,
            grid=(n_seq,),
            in_specs=[pl.BlockSpec((nc, 1), lambda n, pt: (0, 0)),
                      pl.BlockSpec((t_dec, SB_W), rmap), pl.BlockSpec((t_dec, SB_W), rmap),
                      pl.BlockSpec((t_dec, SB_W), rmap)] + page_specs + page_specs,
            out_specs=pl.BlockSpec((t_dec, SB_W), lambda n, pt: (n, 0))),
        out_shape=jax.ShapeDtypeStruct((n_seq * t_dec, SB_W), F32),
        compiler_params=pltpu.CompilerParams(dimension_semantics=("arbitrary",),
                                             vmem_limit_bytes=VMEM_LIMIT),
        name="sb_decode",
    )(pt, bias_row, sq, sk, sv, *([cache_k] * n_pages), *([cache_v] * n_pages))


def _merge_kernel(x_ref, ap_ref, as_ref, hp_ref, hs_ref, ga_ref, gb_ref, wa_ref, wb_ref, wo_ref, o_ref,
                  *, n_prompt_tiles):
    prompt = pl.program_id(0) < n_prompt_tiles
    a = jnp.where(prompt, ap_ref[...], as_ref[...])
    hs = jnp.where(prompt, hp_ref[...], hs_ref[...])
    ya = _dot(a.astype(BF16), wa_ref[...])
    yb = _dot(hs.astype(BF16), wb_ref[...])
    mix = ga_ref[...] * ya + gb_ref[...] * yb
    o_ref[...] = x_ref[...] + _dot(mix.astype(BF16), wo_ref[...])


def _merge(x, a_p, a_s, hs_p, hs_s, ga, gb, wa, wb, wo):
    rows = x.shape[0]
    tm = ROW_TILE
    npt = a_p.shape[0] // tm
    rspec = lambda w: pl.BlockSpec((tm, w), lambda i: (i, 0))
    pspec = lambda w: pl.BlockSpec((tm, w), lambda i: (jnp.minimum(i, npt - 1), 0))
    sspec = lambda w: pl.BlockSpec((tm, w), lambda i: (jnp.maximum(i - npt, 0), 0))
    full = lambda a: pl.BlockSpec(a.shape, lambda i: (0,) * a.ndim)
    return pl.pallas_call(
        functools.partial(_merge_kernel, n_prompt_tiles=npt),
        grid=(rows // tm,),
        in_specs=[rspec(D_MODEL), pspec(M_V_W), sspec(M_V_W), pspec(SB_W), sspec(SB_W),
                  rspec(D_MODEL), rspec(D_MODEL), full(wa), full(wb), full(wo)],
        out_specs=rspec(D_MODEL),
        out_shape=jax.ShapeDtypeStruct((rows, D_MODEL), F32),
        compiler_params=pltpu.CompilerParams(dimension_semantics=("arbitrary",),
                                             vmem_limit_bytes=VMEM_LIMIT),
        name="merge",
    )(x, a_p, a_s, hs_p, hs_s, ga, gb, wa, wb, wo)


_FF_CHUNK = 1024


def _mlp_kernel(x_ref, n2_ref, wu_ref, wd_ref, o_ref):
    x = x_ref[...]
    ms = jnp.mean(x * x, axis=-1, keepdims=True)
    h = (x * lax.rsqrt(ms + EPS) * n2_ref[...]).astype(BF16)
    acc = x
    for c in range(D_FF // _FF_CHUNK):
        u = jnp.maximum(_dot(h, wu_ref[:, c * _FF_CHUNK:(c + 1) * _FF_CHUNK]), 0.0)
        acc = acc + _dot((u * u).astype(BF16), wd_ref[c * _FF_CHUNK:(c + 1) * _FF_CHUNK, :])
    o_ref[...] = acc


def _mlp(x, n2, wu, wd):
    rows = x.shape[0]
    tm = ROW_TILE
    full = lambda a: pl.BlockSpec(a.shape, lambda i: (0,) * a.ndim)
    return pl.pallas_call(
        _mlp_kernel,
        grid=(rows // tm,),
        in_specs=[pl.BlockSpec((tm, D_MODEL), lambda i: (i, 0)), full(n2), full(wu), full(wd)],
        out_specs=pl.BlockSpec((tm, D_MODEL), lambda i: (i, 0)),
        out_shape=jax.ShapeDtypeStruct((rows, D_MODEL), F32),
        compiler_params=pltpu.CompilerParams(dimension_semantics=("arbitrary",),
                                             vmem_limit_bytes=VMEM_LIMIT),
        name="mlp",
    )(x, n2, wu, wd)


def _round_up(n, m):
    return (n + m - 1) // m * m


def kernel(x_prompt, x_sample, cache_k, cache_v, state_C, state_n, state_m, page_table, meta_tokens,
           norm1, w_in, b_if, sb_q_gain, sb_k_gain, sb_logit_bias, mlstm_gain, w_br_a, w_br_b, w_o,
           norm2, w_up, w_down):
    n_batch, seq = x_prompt.shape[:2]
    n_dec, t_dec = x_sample.shape[:2]
    depth = w_in.shape[0]
    t_prompt = N_META + seq
    tp = _round_up(t_prompt, max(SB_BLOCK, M_CHUNK, ROW_TILE))
    n_sample_rows = n_dec * t_dec
    row_s = n_batch * tp
    rows = _round_up(row_s + n_sample_rows, ROW_TILE)

    meta = jnp.broadcast_to(meta_tokens[None], (n_batch, N_META, D_MODEL))
    xp = jnp.concatenate([meta, x_prompt, jnp.zeros((n_batch, tp - t_prompt, D_MODEL), F32)], axis=1)
    x = jnp.concatenate([xp.reshape(row_s, D_MODEL), x_sample.reshape(n_sample_rows, D_MODEL),
                         jnp.zeros((rows - row_s - n_sample_rows, D_MODEL), F32)], axis=0)

    n_pool, page = cache_k.shape[1:3]
    cache_k = jnp.transpose(cache_k, (0, 1, 3, 4, 2)).reshape(depth, n_pool, SB_W, page)
    cache_v = jnp.transpose(cache_v, (0, 1, 3, 4, 2)).reshape(depth, n_pool, SB_W, page)
    c_gate = 2 * M_QK_W + 2 * M_V_W
    zeros_C = jnp.zeros((n_batch, M_HEADS, M_QK_DIM, M_V_DIM), F32)
    state_Ct = jnp.swapaxes(state_C, -1, -2)
    zeros_n = jnp.zeros((n_batch, M_HEADS, M_QK_DIM), F32)
    zeros_m = jnp.zeros((n_batch, M_HEADS), F32)
    tail_pad = ((0, rows - row_s - n_sample_rows), (0, 0))

    fill_pos = jnp.asarray([h * PAIR_W + (SB_HEAD_DIM if h % 2 == 0 else 0) for h in range(SB_HEADS)])
    outs = [[] for _ in range(10)]
    for l in range(depth):
        wm = jnp.concatenate([w_in[l, :, :c_gate], w_in[l, :, c_gate + 2 * M_HEADS:]], axis=1).astype(BF16)
        wg = jnp.pad(w_in[l, :, c_gate:c_gate + 2 * M_HEADS], ((0, 0), (0, GATE_W - 2 * M_HEADS))).astype(BF16)
        bg = jnp.pad(b_if[l], (0, GATE_W - 2 * M_HEADS)).reshape(1, GATE_W)
        qg = jnp.tile(sb_q_gain[l], SB_HEADS).reshape(1, SB_W)
        kg = jnp.tile(sb_k_gain[l], SB_HEADS).reshape(1, SB_W)
        bias2 = sb_logit_bias[l] * LOG2E
        bias_hi = bias2.astype(BF16).astype(F32)
        zero_row = jnp.zeros((SB_HEADS * PAIR_W,), F32)
        fill = jnp.stack([zero_row.at[fill_pos].set(1.0).at[fill_pos + 1].set(1.0),
                          zero_row.at[fill_pos].set(bias_hi).at[fill_pos + 1].set(bias2 - bias_hi),
                          zero_row])
        mq, mk, mv, mo, g, sq, sk, sv, ga, gb, sqb, skb, svb, kt, vt = _proj(
            x, norm1[l].reshape(1, D_MODEL), wm, wg, bg, qg, kg, fill,
            n_batch=n_batch, tp=tp, t_prompt=t_prompt)

        a_p, C_p, n_p, m_p = _mlstm(mq, mk, mv, g, mo, mlstm_gain[l], zeros_C, zeros_n, zeros_m,
                                    row0=0, n_seq=n_batch, nc=tp // M_CHUNK, L=M_CHUNK, t_real=t_prompt)
        a_s, C_s, n_s, m_s = _mlstm(mq, mk, mv, g, mo, mlstm_gain[l], state_Ct[l], state_n[l], state_m[l],
                                    row0=row_s, n_seq=n_dec, nc=1, L=t_dec, t_real=t_dec)
        hs_p = _sb_prompt(sqb, skb, svb, n_batch=n_batch, tp=tp)
        hs_s = _sb_decode(sq, sk, sv, cache_k, cache_v, page_table, bias2,
                          layer=l, row0=row_s, t_dec=t_dec)
        x = _merge(x, a_p, jnp.pad(a_s, tail_pad), hs_p, jnp.pad(hs_s, tail_pad), ga, gb,
                   w_br_a[l].astype(BF16), w_br_b[l].astype(BF16), w_o[l].astype(BF16))
        x = _mlp(x, norm2[l].reshape(1, D_MODEL), w_up[l].astype(BF16), w_down[l].astype(BF16))

        kv_s = lambda t: t[row_s:row_s + n_sample_rows].reshape(n_dec, t_dec, SB_HEADS, SB_HEAD_DIM)
        for lst, val in zip(outs, (kt, vt, C_p, n_p, m_p, kv_s(sk), kv_s(sv), C_s, n_s, m_s)):
            lst.append(val)

    y_prompt = x[:row_s].reshape(n_batch, tp, D_MODEL)[:, N_META:t_prompt]
    y_sample = x[row_s:row_s + n_sample_rows].reshape(n_dec, t_dec, D_MODEL)
    outs = [jnp.stack(o) for o in outs]
    outs[0], outs[1] = (jnp.transpose(t, (0, 1, 4, 2, 3)) for t in outs[:2])
    outs[2], outs[7] = jnp.swapaxes(outs[2], -1, -2), jnp.swapaxes(outs[7], -1, -2)
    return (y_prompt, y_sample) + tuple(outs)
```

```python
import functools

import jax
import jax.numpy as jnp
from jax import lax
from jax.experimental import pallas as pl
from jax.experimental.pallas import tpu as pltpu

F32 = jnp.float32
BF16 = jnp.bfloat16
HIGHEST = lax.Precision.HIGHEST

D_MODEL = 1024
N_META = 16
M_HEADS = 4
M_V_DIM = 128
M_QK_DIM = 64
SB_HEADS = 8
SB_HEAD_DIM = 64
D_FF = 4 * D_MODEL
M_QK_W = M_HEADS * M_QK_DIM
M_V_W = M_HEADS * M_V_DIM
SB_W = SB_HEADS * SB_HEAD_DIM
EPS = 1e-6
NEG = -1e30
LOG2E = 1.4426950408889634
GATE_W = 128
PAIR_W = 2 * SB_HEAD_DIM

ROW_TILE = 256
M_CHUNK = 256
SB_BLOCK = 256
SB_KEYS = 3
SB_STAGGER = 1
VMEM_LIMIT = 56 * 1024 * 1024

_C_MQ, _C_MK, _C_MV, _C_MO = 0, 256, 512, 1024
_C_SQ, _C_SK, _C_SV, _C_GA, _C_GB, _C_END = 1536, 2048, 2560, 3072, 4096, 5120


def _softplus(z):
    return jnp.maximum(z, 0.0) + jnp.log1p(jnp.exp(-jnp.abs(z)))


def _softplus2(z):
    neg_abs = pltpu.bitcast(pltpu.bitcast(z, jnp.uint32) | jnp.uint32(0x80000000), F32)
    return jnp.maximum(z, 0.0) + jnp.log(1.0 + jnp.exp2(neg_abs)) * LOG2E


def _sigmoid(z):
    return 1.0 / (1.0 + jnp.exp(-z))


def _split_bf16(x):
    hi = x.astype(BF16)
    lo = (x - hi.astype(F32)).astype(BF16)
    return hi, lo


def _dot(a, b):
    return jnp.dot(a, b, preferred_element_type=F32)


def _dot_nt(a, b):
    return lax.dot_general(a, b, (((1,), (1,)), ((), ())), preferred_element_type=F32)


def _dot_tn(a, b):
    return lax.dot_general(a, b, (((0,), (0,)), ((), ())), preferred_element_type=F32)


def _head_norm(y, gain_full):
    y2 = y * y
    head = lax.broadcasted_iota(jnp.int32, (1, SB_W), 1) // SB_HEAD_DIM
    scale = jnp.zeros_like(y)
    for h in range(SB_HEADS):
        m = head == h
        ssq = jnp.sum(jnp.where(m, y2, 0.0), axis=-1, keepdims=True)
        r = lax.rsqrt(ssq * (1.0 / SB_HEAD_DIM) + EPS)
        scale = jnp.where(m, r, scale)
    return y * scale * gain_full


def _proj_kernel(x_ref, n1_ref, wm_ref, wg_ref, bg_ref, qg_ref, kg_ref, fill_ref,
                 mq_ref, mk_ref, mv_ref, mo_ref, g_ref, sq_ref, sk_ref, sv_ref, ga_ref, gb_ref,
                 sqb_ref, skb_ref, svb_ref, kt_ref, vt_ref, *, n_prompt_tiles):
    x = x_ref[...]
    ms = jnp.mean(x * x, axis=-1, keepdims=True)
    h = (x * lax.rsqrt(ms + EPS) * n1_ref[...]).astype(BF16)

    def mm(lo, hi):
        return _dot(h, wm_ref[:, lo:hi])

    mq_ref[...] = mm(_C_MQ, _C_MK)
    mk_ref[...] = mm(_C_MK, _C_MV) * (M_QK_DIM ** -0.5)
    mv_ref[...] = mm(_C_MV, _C_MO)
    mo_ref[...] = _sigmoid(mm(_C_MO, _C_SQ))
    g = _dot(h, wg_ref[...]) + bg_ref[...]
    lane = lax.broadcasted_iota(jnp.int32, (1, GATE_W), 1)
    g_ref[...] = jnp.where(lane < M_HEADS, g, -_softplus(-g))
    sq = _head_norm(mm(_C_SQ, _C_SK), qg_ref[...]) * (SB_HEAD_DIM ** -0.5 * LOG2E)
    sk = _head_norm(mm(_C_SK, _C_SV), kg_ref[...])
    sv = mm(_C_SV, _C_GA)
    sq_ref[...], sk_ref[...], sv_ref[...] = sq, sk, sv

    @pl.when(pl.program_id(0) < n_prompt_tiles)
    def _():
        kt_ref[...] = sk.T.reshape(SB_HEADS, SB_HEAD_DIM, sk.shape[0])
        vt_ref[...] = sv.T.reshape(SB_HEADS, SB_HEAD_DIM, sv.shape[0])

    first = lax.broadcasted_iota(jnp.int32, (1, PAIR_W), 1) < SB_HEAD_DIM
    for src, dst, row in ((sq, sqb_ref, 0), (sk, skb_ref, 1), (sv, svb_ref, 2)):
        for head in range(SB_HEADS):
            grp = slice(head * PAIR_W, (head + 1) * PAIR_W)
            pair = src[:, (head // 2) * PAIR_W:(head // 2 + 1) * PAIR_W]
            own = first if head % 2 == 0 else jnp.logical_not(first)
            dst[:, grp] = jnp.where(own, pair, fill_ref[row:row + 1, grp]).astype(BF16)
    ga_ref[...] = _sigmoid(mm(_C_GA, _C_GB))
    gb_ref[...] = _sigmoid(mm(_C_GB, _C_END))


def _proj(x, n1, wm, wg, bg, qg, kg, fill, *, n_batch, tp, t_prompt):
    rows = x.shape[0]
    tm = ROW_TILE
    nt = tp // tm
    widths = (M_QK_W, M_QK_W, M_V_W, M_V_W, GATE_W, SB_W, SB_W, SB_W, D_MODEL, D_MODEL)
    bf_widths = (SB_HEADS * PAIR_W,) * 3
    full = lambda a: pl.BlockSpec(a.shape, lambda i: (0,) * a.ndim)
    last = n_batch * nt - 1

    def kv_map(i):
        i = jnp.minimum(i, last)
        return (i // nt, 0, 0, i % nt)

    kv_spec = pl.BlockSpec((None, SB_HEADS, SB_HEAD_DIM, tm), kv_map)
    kv_shape = jax.ShapeDtypeStruct((n_batch, SB_HEADS, SB_HEAD_DIM, t_prompt), F32)
    return pl.pallas_call(
        functools.partial(_proj_kernel, n_prompt_tiles=n_batch * nt),
        grid=(rows // tm,),
        in_specs=[pl.BlockSpec((tm, D_MODEL), lambda i: (i, 0)),
                  full(n1), full(wm), full(wg), full(bg), full(qg), full(kg), full(fill)],
        out_specs=[pl.BlockSpec((tm, w), lambda i: (i, 0)) for w in widths + bf_widths] + [kv_spec] * 2,
        out_shape=[jax.ShapeDtypeStruct((rows, w), F32) for w in widths]
        + [jax.ShapeDtypeStruct((rows, w), BF16) for w in bf_widths] + [kv_shape] * 2,
        compiler_params=pltpu.CompilerParams(dimension_semantics=("arbitrary",),
                                             vmem_limit_bytes=VMEM_LIMIT),
        name="proj",
    )(x, n1, wm, wg, bg, qg, kg, fill)


def _mlstm_kernel(q_ref, k_ref, v_ref, g_ref, mo_ref, gain_ref, C0_ref, n0_ref, m0_ref,
                  a_ref, Co_ref, no_ref, mout_ref, C_scr, n_scr, m_scr, *, L, t_real, nc, G):
    c = pl.program_id(1)

    @pl.when(c == 0)
    def _():
        C_scr[...] = C0_ref[...]
        n_scr[...] = n0_ref[...]
        m_scr[...] = m0_ref[...]

    mxu = BF16 if L >= 128 else F32
    row = lax.broadcasted_iota(jnp.int32, (L, 1), 0)
    lane = lax.broadcasted_iota(jnp.int32, (1, GATE_W), 1)
    valid = (c * L + row) < t_real
    ti = lax.broadcasted_iota(jnp.int32, (L, L), 0)
    si = lax.broadcasted_iota(jnp.int32, (L, L), 1)
    causal = si <= ti
    eye = si == ti

    units = [(seq, h) for seq in range(G) for h in range(M_HEADS)]
    gates, live = {}, {}

    def weights_and_scores(seq, h):
        rs = slice(seq * L, (seq + 1) * L)
        if seq not in gates:
            g = jnp.where(valid, g_ref[rs, :], jnp.where(lane < M_HEADS, NEG, 0.0))
            gates[seq] = g, jnp.dot(jnp.where(causal, 1.0, 0.0), g, precision=HIGHEST,
                                    preferred_element_type=F32)
        g, b_all = gates[seq]
        qf = q_ref[rs, h * M_QK_DIM:(h + 1) * M_QK_DIM]
        kf = k_ref[rs, h * M_QK_DIM:(h + 1) * M_QK_DIM]
        q, k = qf.astype(mxu), kf.astype(mxu)
        m0 = m_scr[seq, 0:1, h:h + 1]
        logi = g[:, h:h + 1]
        b = b_all[:, M_HEADS + h:M_HEADS + h + 1]
        r_col = logi - b
        r_row = jnp.sum(jnp.where(eye, r_col, 0.0), axis=0, keepdims=True)
        dm = jnp.where(causal, b + r_row, NEG)
        inter = b + m0
        m = jnp.maximum(inter, jnp.max(dm, axis=1, keepdims=True))
        s = _dot_nt(q, k) * jnp.exp(dm - m)
        live[seq, h] = dict(rs=rs, qf=qf, kf=kf, q=q, k=k, m0=m0, logi=logi, b=b, m=m, s=s,
                            e_inter=jnp.exp(inter - m))

    def outputs(seq, h):
        d = live[seq, h]
        rs, s, m, e_inter = d["rs"], d["s"], d["m"], d["e_inter"]
        vf = v_ref[rs, h * M_V_DIM:(h + 1) * M_V_DIM]
        C0 = C_scr[seq, h]
        n0 = n_scr[seq, h:h + 1, :]
        num = _dot(s.astype(mxu), vf.astype(mxu)) + e_inter * _dot(d["q"], C0.astype(mxu))
        den = jnp.sum(s, axis=1, keepdims=True) + e_inter * jnp.sum(d["qf"] * n0, axis=1, keepdims=True)
        hh = num / jnp.maximum(jnp.abs(den), jnp.exp(-m))
        hn = hh * lax.rsqrt(jnp.mean(hh * hh, axis=-1, keepdims=True) + EPS) * gain_ref[h:h + 1, :]
        a_ref[rs, h * M_V_DIM:(h + 1) * M_V_DIM] = hn * mo_ref[rs, h * M_V_DIM:(h + 1) * M_V_DIM]
        d.update(vf=vf, C0=C0, n0=n0)

    def state_update(seq, h):
        d = live.pop((seq, h))
        m, b, m0 = d["m"], d["b"], d["m0"]
        m_new = m[L - 1:L, :]
        b_last = b[L - 1:L, :]
        w = jnp.exp(b_last - b + d["logi"] - m_new)
        decay = jnp.exp(b_last + m0 - m_new)
        C_scr[seq, h] = decay * d["C0"] + _dot_tn(d["k"], (d["vf"] * w).astype(mxu))
        n_scr[seq, h:h + 1, :] = decay * d["n0"] + jnp.sum(w * d["kf"], axis=0, keepdims=True)
        m_scr[seq, 0:1, h:h + 1] = m_new

    lag = 1 if L < 128 else 0
    for step in range(len(units) + 2 * lag):
        if step < len(units):
            weights_and_scores(*units[step])
        if 0 <= step - lag < len(units):
            outputs(*units[step - lag])
        if 0 <= step - 2 * lag < len(units):
            state_update(*units[step - 2 * lag])

    @pl.when(c == nc - 1)
    def _():
        Co_ref[...] = C_scr[...]
        no_ref[...] = n_scr[...]
        mout_ref[...] = m_scr[...]


def _mlstm(mq, mk, mv, g, mo, gain, C0, n0, m0, *, row0, n_seq, nc, L, t_real):
    G = 1 if nc > 1 else max(d for d in (1, 2, 4, 8) if n_seq % d == 0)
    blk0 = row0 // (G * L)
    rmap = lambda s, c: (blk0 + s * nc + c, 0)
    smap3 = lambda s, c: (s, 0, 0)
    smap4 = lambda s, c: (s, 0, 0, 0)
    m0 = m0.reshape(n_seq, 1, M_HEADS)
    kern = functools.partial(_mlstm_kernel, L=L, t_real=t_real, nc=nc, G=G)
    a, Co, no, mo_out = pl.pallas_call(
        kern,
        grid=(n_seq // G, nc),
        in_specs=[pl.BlockSpec((G * L, M_QK_W), rmap), pl.BlockSpec((G * L, M_QK_W), rmap),
                  pl.BlockSpec((G * L, M_V_W), rmap), pl.BlockSpec((G * L, GATE_W), rmap),
                  pl.BlockSpec((G * L, M_V_W), rmap),
                  pl.BlockSpec((M_HEADS, M_V_DIM), lambda s, c: (0, 0)),
                  pl.BlockSpec((G, M_HEADS, M_QK_DIM, M_V_DIM), smap4),
                  pl.BlockSpec((G, M_HEADS, M_QK_DIM), smap3),
                  pl.BlockSpec((G, 1, M_HEADS), smap3)],
        out_specs=[pl.BlockSpec((G * L, M_V_W), lambda s, c: (s * nc + c, 0)),
                   pl.BlockSpec((G, M_HEADS, M_QK_DIM, M_V_DIM), smap4),
                   pl.BlockSpec((G, M_HEADS, M_QK_DIM), smap3),
                   pl.BlockSpec((G, 1, M_HEADS), smap3)],
        out_shape=[jax.ShapeDtypeStruct((n_seq * nc * L, M_V_W), F32),
                   jax.ShapeDtypeStruct((n_seq, M_HEADS, M_QK_DIM, M_V_DIM), F32),
                   jax.ShapeDtypeStruct((n_seq, M_HEADS, M_QK_DIM), F32),
                   jax.ShapeDtypeStruct((n_seq, 1, M_HEADS), F32)],
        scratch_shapes=[pltpu.VMEM((G, M_HEADS, M_QK_DIM, M_V_DIM), F32),
                        pltpu.VMEM((G, M_HEADS, M_QK_DIM), F32),
                        pltpu.VMEM((G, 1, M_HEADS), F32)],
        compiler_params=pltpu.CompilerParams(dimension_semantics=("arbitrary", "arbitrary"),
                                             vmem_limit_bytes=VMEM_LIMIT),
        name="mlstm_L%d" % L,
    )(mq, mk, mv, g, mo, gain, C0, n0, m0)
    return a, Co, no, mo_out.reshape(n_seq, M_HEADS)


def _sb_kernel(qi_ref, kj_ref, nk_ref, q_ref, *refs, T):
    kv_refs = refs[:2 * SB_KEYS]
    o_ref, acc_ref, carry_ref = refs[2 * SB_KEYS:]
    p = pl.program_id(1)
    i = qi_ref[p]
    j1 = kj_ref[p]
    nk = nk_ref[p]

    @pl.when(j1 == i)
    def _():
        acc_ref[...] = jnp.zeros_like(acc_ref)
        carry_ref[...] = jnp.zeros_like(carry_ref)

    def body(diag, n_blocks):
        ti = lax.broadcasted_iota(jnp.int32, (T, T), 0)
        si = lax.broadcasted_iota(jnp.int32, (T, T), 1)
        upper = jnp.where(ti > si, 1.0, 0.0).astype(BF16)
        valid = si < ti
        blocks = [(kv_refs[2 * n], kv_refs[2 * n + 1], diag and n == 0) for n in range(n_blocks)]
        units = [(h, blk) for blk in range(len(blocks)) for h in range(SB_HEADS)]
        sls = [slice((h // 2) * PAIR_W, (h // 2 + 1) * PAIR_W) for h in range(SB_HEADS)]
        own = [slice(h * PAIR_W, (h + 1) * PAIR_W) for h in range(SB_HEADS)]
        zs, rest, behind = {}, {}, {}
        carries = [carry_ref[h] for h in range(SB_HEADS)]
        outs = [None] * SB_HEADS

        def scores(h, blk):
            zs[h, blk] = _dot_nt(q_ref[:, own[h]], blocks[blk][0][:, own[h]])

        def sums(h, blk):
            z = zs.pop((h, blk))
            sp = _softplus2(z)
            if blocks[blk][2]:
                sp = jnp.where(valid, sp, 0.0)
            after = _dot(sp.astype(BF16), upper)
            behind[h, blk] = after + jnp.concatenate([carries[h]] * (T // PAIR_W), axis=1)
            rest[h, blk] = z - sp
            carries[h] = carries[h] + (after[:, 0:1] + sp[:, 0:1])

        def weights(h, blk):
            a = jnp.exp2(rest.pop((h, blk)) - behind.pop((h, blk)))
            if blocks[blk][2]:
                a = jnp.where(valid, a, 0.0)
            o = _dot(a.astype(BF16), blocks[blk][1][:, own[h]])
            outs[h] = o if outs[h] is None else outs[h] + o

        for s in range(len(units) + 2 * SB_STAGGER):
            if s < len(units):
                scores(*units[s])
            if 0 <= s - SB_STAGGER < len(units):
                sums(*units[s - SB_STAGGER])
            if 0 <= s - 2 * SB_STAGGER < len(units):
                weights(*units[s - 2 * SB_STAGGER])
        for h in range(SB_HEADS):
            carry_ref[h] = carries[h]
        for pair in range(SB_HEADS // 2):
            acc_ref[:, sls[2 * pair]] += outs[2 * pair] + outs[2 * pair + 1]

    for diag in (True, False):
        for n_blocks in range(1, SB_KEYS + 1):
            cond = ((j1 == i) if diag else (j1 < i)) & (nk == n_blocks)
            pl.when(cond)(functools.partial(body, diag, n_blocks))

    @pl.when(j1 - nk + 1 == 0)
    def _():
        o_ref[...] = acc_ref[...]


def _sb_prompt(sq, sk, sv, *, n_batch, tp):
    T = SB_BLOCK
    nq = tp // T
    steps = [(i, j, min(SB_KEYS, j + 1)) for i in range(nq) for j in range(i, -1, -SB_KEYS)]
    qi, kj, nk = (jnp.asarray(c, jnp.int32) for c in zip(*steps))
    width = SB_HEADS * PAIR_W
    qmap = lambda b, p, qi, kj, nk: (b * nq + qi[p], 0)

    def kmap(n):
        return lambda b, p, qi, kj, nk: (b * nq + jnp.maximum(kj[p] - n, 0), 0)

    kv_specs = [pl.BlockSpec((T, width), kmap(n)) for n in range(SB_KEYS) for _ in range(2)]
    return pl.pallas_call(
        functools.partial(_sb_kernel, T=T),
        grid_spec=pltpu.PrefetchScalarGridSpec(
            num_scalar_prefetch=3,
            grid=(n_batch, len(steps)),
            in_specs=[pl.BlockSpec((T, width), qmap)] + kv_specs,
            out_specs=pl.BlockSpec((T, SB_W), qmap),
            scratch_shapes=[pltpu.VMEM((T, SB_W), F32), pltpu.VMEM((SB_HEADS, T, PAIR_W), F32)]),
        out_shape=jax.ShapeDtypeStruct((n_batch * tp, SB_W), F32),
        compiler_params=pltpu.CompilerParams(dimension_semantics=("arbitrary", "arbitrary"),
                                             vmem_limit_bytes=VMEM_LIMIT),
        name="sb_prompt",
    )(qi, kj, nk, sq, *([sk, sv] * SB_KEYS))


def _sb_decode_kernel(pt_ref, bias_ref, q_ref, kc_ref, vc_ref, *refs, n_pages, page, t_dec):
    del pt_ref
    k_pages = refs[:n_pages]
    v_pages = refs[n_pages:2 * n_pages]
    o_ref = refs[2 * n_pages]
    nc = SB_HEADS * t_dec
    q = q_ref[...]
    crow = lax.broadcasted_iota(jnp.int32, (nc, SB_W), 0)
    clane = lax.broadcasted_iota(jnp.int32, (nc, SB_W), 1)
    own_head = (crow // t_dec) == (clane // SB_HEAD_DIM)
    q_bd = jnp.where(own_head, jnp.concatenate([q] * SB_HEADS, axis=0), 0.0).astype(BF16)
    bias = bias_ref[...]
    ji = lax.broadcasted_iota(jnp.int32, (page, page), 0)
    si = lax.broadcasted_iota(jnp.int32, (page, page), 1)
    upper = jnp.where(ji > si, 1.0, 0.0).astype(BF16)

    pad = jnp.zeros((page - t_dec, SB_W), F32)
    kc = jnp.concatenate([kc_ref[...], pad], axis=0).astype(BF16)
    vc = jnp.concatenate([vc_ref[...], pad], axis=0).astype(BF16)
    t_idx = lax.broadcasted_iota(jnp.int32, (nc, page), 0) % t_dec
    s_idx = lax.broadcasted_iota(jnp.int32, (nc, page), 1)
    valid = s_idx < t_idx
    order = list(range(n_pages - 1, -1, -1))
    zs = [_dot_nt(q_bd, kc) + bias] + [_dot(q_bd, k_pages[pg][...].astype(BF16)) + bias for pg in order]
    rests, behinds = [], []
    carry = jnp.zeros((nc, 1), F32)
    for n, z in enumerate(zs):
        sp = _softplus2(z)
        if n == 0:
            sp = jnp.where(valid, sp, 0.0)
        behinds.append(_dot(sp.astype(BF16), upper) + carry)
        rests.append(z - sp)
        carry = carry + jnp.sum(sp, axis=1, keepdims=True)
    acc = jnp.zeros((nc, SB_W), F32)
    for n, (rest, behind) in enumerate(zip(rests, behinds)):
        a = jnp.exp2(rest - behind)
        if n == 0:
            a = jnp.where(valid, a, 0.0).astype(BF16)
            acc = acc + _dot(a, vc)
        else:
            acc = acc + _dot_nt(a.astype(BF16), v_pages[order[n - 1]][...].astype(BF16))

    olane = lax.broadcasted_iota(jnp.int32, (t_dec, SB_W), 1) // SB_HEAD_DIM
    out = jnp.zeros((t_dec, SB_W), F32)
    for h in range(SB_HEADS):
        out = out + jnp.where(olane == h, acc[h * t_dec:(h + 1) * t_dec, :], 0.0)
    o_ref[...] = out


def _sb_decode(sq, sk, sv, cache_k, cache_v, page_table, bias, *, layer, row0, t_dec):
    n_seq, n_pages = page_table.shape
    page = cache_k.shape[3]
    blk0 = row0 // t_dec
    nc = SB_HEADS * t_dec
    bias_row = jnp.repeat(bias, t_dec).reshape(nc, 1)
    pt = page_table.reshape(-1)
    rmap = lambda n, pt: (blk0 + n, 0)

    def pmap(pg):
        return lambda n, pt: (layer, pt[n * n_pages + pg], 0, 0)

    page_specs = [pl.BlockSpec((None, None, SB_W, page), pmap(pg)) for pg in range(n_pages)]
    kern = functools.partial(_sb_decode_kernel, n_pages=n_pages, page=page, t_dec=t_dec)
    return pl.pallas_call(
        kern,
        grid_spec=pltpu.PrefetchScalarGridSpec(
            num_scalar_prefetch=1,
            grid=(n_seq,),
            in_specs=[pl.BlockSpec((nc, 1), lambda n, pt: (0, 0)),
                      pl.BlockSpec((t_dec, SB_W), rmap), pl.BlockSpec((t_dec, SB_W), rmap),
                      pl.BlockSpec((t_dec, SB_W), rmap)] + page_specs + page_specs,
            out_specs=pl.BlockSpec((t_dec, SB_W), lambda n, pt: (n, 0))),
        out_shape=jax.ShapeDtypeStruct((n_seq * t_dec, SB_W), F32),
        compiler_params=pltpu.CompilerParams(dimension_semantics=("arbitrary",),
                                             vmem_limit_bytes=VMEM_LIMIT),
        name="sb_decode",
    )(pt, bias_row, sq, sk, sv, *([cache_k] * n_pages), *([cache_v] * n_pages))


def _merge_kernel(x_ref, ap_ref, as_ref, hp_ref, hs_ref, ga_ref, gb_ref, wa_ref, wb_ref, wo_ref, o_ref,
                  *, n_prompt_tiles):
    prompt = pl.program_id(0) < n_prompt_tiles
    a = jnp.where(prompt, ap_ref[...], as_ref[...])
    hs = jnp.where(prompt, hp_ref[...], hs_ref[...])
    ya = _dot(a.astype(BF16), wa_ref[...])
    yb = _dot(hs.astype(BF16), wb_ref[...])
    mix = ga_ref[...] * ya + gb_ref[...] * yb
    o_ref[...] = x_ref[...] + _dot(mix.astype(BF16), wo_ref[...])


def _merge(x, a_p, a_s, hs_p, hs_s, ga, gb, wa, wb, wo):
    rows = x.shape[0]
    tm = ROW_TILE
    npt = a_p.shape[0] // tm
    rspec = lambda w: pl.BlockSpec((tm, w), lambda i: (i, 0))
    pspec = lambda w: pl.BlockSpec((tm, w), lambda i: (jnp.minimum(i, npt - 1), 0))
    sspec = lambda w: pl.BlockSpec((tm, w), lambda i: (jnp.maximum(i - npt, 0), 0))
    full = lambda a: pl.BlockSpec(a.shape, lambda i: (0,) * a.ndim)
    return pl.pallas_call(
        functools.partial(_merge_kernel, n_prompt_tiles=npt),
        grid=(rows // tm,),
        in_specs=[rspec(D_MODEL), pspec(M_V_W), sspec(M_V_W), pspec(SB_W), sspec(SB_W),
                  rspec(D_MODEL), rspec(D_MODEL), full(wa), full(wb), full(wo)],
        out_specs=rspec(D_MODEL),
        out_shape=jax.ShapeDtypeStruct((rows, D_MODEL), F32),
        compiler_params=pltpu.CompilerParams(dimension_semantics=("arbitrary",),
                                             vmem_limit_bytes=VMEM_LIMIT),
        name="merge",
    )(x, a_p, a_s, hs_p, hs_s, ga, gb, wa, wb, wo)


_FF_CHUNK = 1024


def _mlp_kernel(x_ref, n2_ref, wu_ref, wd_ref, o_ref):
    x = x_ref[...]
    ms = jnp.mean(x * x, axis=-1, keepdims=True)
    h = (x * lax.rsqrt(ms + EPS) * n2_ref[...]).astype(BF16)
    acc = x
    for c in range(D_FF // _FF_CHUNK):
        u = jnp.maximum(_dot(h, wu_ref[:, c * _FF_CHUNK:(c + 1) * _FF_CHUNK]), 0.0)
        acc = acc + _dot((u * u).astype(BF16), wd_ref[c * _FF_CHUNK:(c + 1) * _FF_CHUNK, :])
    o_ref[...] = acc


def _mlp(x, n2, wu, wd):
    rows = x.shape[0]
    tm = ROW_TILE
    full = lambda a: pl.BlockSpec(a.shape, lambda i: (0,) * a.ndim)
    return pl.pallas_call(
        _mlp_kernel,
        grid=(rows // tm,),
        in_specs=[pl.BlockSpec((tm, D_MODEL), lambda i: (i, 0)), full(n2), full(wu), full(wd)],
        out_specs=pl.BlockSpec((tm, D_MODEL), lambda i: (i, 0)),
        out_shape=jax.ShapeDtypeStruct((rows, D_MODEL), F32),
        compiler_params=pltpu.CompilerParams(dimension_semantics=("arbitrary",),
                                             vmem_limit_bytes=VMEM_LIMIT),
        name="mlp",
    )(x, n2, wu, wd)


def _round_up(n, m):
    return (n + m - 1) // m * m


def kernel(x_prompt, x_sample, cache_k, cache_v, state_C, state_n, state_m, page_table, meta_tokens,
           norm1, w_in, b_if, sb_q_gain, sb_k_gain, sb_logit_bias, mlstm_gain, w_br_a, w_br_b, w_o,
           norm2, w_up, w_down):
    n_batch, seq = x_prompt.shape[:2]
    n_dec, t_dec = x_sample.shape[:2]
    depth = w_in.shape[0]
    t_prompt = N_META + seq
    tp = _round_up(t_prompt, max(SB_BLOCK, M_CHUNK, ROW_TILE))
    n_sample_rows = n_dec * t_dec
    row_s = n_batch * tp
    rows = _round_up(row_s + n_sample_rows, ROW_TILE)

    meta = jnp.broadcast_to(meta_tokens[None], (n_batch, N_META, D_MODEL))
    xp = jnp.concatenate([meta, x_prompt, jnp.zeros((n_batch, tp - t_prompt, D_MODEL), F32)], axis=1)
    x = jnp.concatenate([xp.reshape(row_s, D_MODEL), x_sample.reshape(n_sample_rows, D_MODEL),
                         jnp.zeros((rows - row_s - n_sample_rows, D_MODEL), F32)], axis=0)

    n_pool, page = cache_k.shape[1:3]
    cache_k = jnp.transpose(cache_k, (0, 1, 3, 4, 2)).reshape(depth, n_pool, SB_W, page)
    cache_v = jnp.transpose(cache_v, (0, 1, 3, 4, 2)).reshape(depth, n_pool, SB_W, page)
    c_gate = 2 * M_QK_W + 2 * M_V_W
    zeros_C = jnp.zeros((n_batch, M_HEADS, M_QK_DIM, M_V_DIM), F32)
    state_Ct = jnp.swapaxes(state_C, -1, -2)
    zeros_n = jnp.zeros((n_batch, M_HEADS, M_QK_DIM), F32)
    zeros_m = jnp.zeros((n_batch, M_HEADS), F32)
    tail_pad = ((0, rows - row_s - n_sample_rows), (0, 0))

    fill_pos = jnp.asarray([h * PAIR_W + (SB_HEAD_DIM if h % 2 == 0 else 0) for h in range(SB_HEADS)])
    outs = [[] for _ in range(10)]
    for l in range(depth):
        wm = jnp.concatenate([w_in[l, :, :c_gate], w_in[l, :, c_gate + 2 * M_HEADS:]], axis=1).astype(BF16)
        wg = jnp.pad(w_in[l, :, c_gate:c_gate + 2 * M_HEADS], ((0, 0), (0, GATE_W - 2 * M_HEADS))).astype(BF16)
        bg = jnp.pad(b_if[l], (0, GATE_W - 2 * M_HEADS)).reshape(1, GATE_W)
        qg = jnp.tile(sb_q_gain[l], SB_HEADS).reshape(1, SB_W)
        kg = jnp.tile(sb_k_gain[l], SB_HEADS).reshape(1, SB_W)
        bias2 = sb_logit_bias[l] * LOG2E
        bias_hi = bias2.astype(BF16).astype(F32)
        zero_row = jnp.zeros((SB_HEADS * PAIR_W,), F32)
        fill = jnp.stack([zero_row.at[fill_pos].set(1.0).at[fill_pos + 1].set(1.0),
                          zero_row.at[fill_pos].set(bias_hi).at[fill_pos + 1].set(bias2 - bias_hi),
                          zero_row])
        mq, mk, mv, mo, g, sq, sk, sv, ga, gb, sqb, skb, svb, kt, vt = _proj(
            x, norm1[l].reshape(1, D_MODEL), wm, wg, bg, qg, kg, fill,
            n_batch=n_batch, tp=tp, t_prompt=t_prompt)

        a_p, C_p, n_p, m_p = _mlstm(mq, mk, mv, g, mo, mlstm_gain[l], zeros_C, zeros_n, zeros_m,
                                    row0=0, n_seq=n_batch, nc=tp // M_CHUNK, L=M_CHUNK, t_real=t_prompt)
        a_s, C_s, n_s, m_s = _mlstm(mq, mk, mv, g, mo, mlstm_gain[l], state_Ct[l], state_n[l], state_m[l],
                                    row0=row_s, n_seq=n_dec, nc=1, L=t_dec, t_real=t_dec)
        hs_p = _sb_prompt(sqb, skb, svb, n_batch=n_batch, tp=tp)
        hs_s = _sb_decode(sq, sk, sv, cache_k, cache_v, page_table, bias2,
                          layer=l, row0=row_s, t_dec=t_dec)
        x = _merge(x, a_p, jnp.pad(a_s, tail_pad), hs_p, jnp.pad(hs_s, tail_pad), ga, gb,
                   w_br_a[l].astype(BF16), w_br_b[l].astype(BF16), w_o[l].astype(BF16))
        x = _mlp(x, norm2[l].reshape(1, D_MODEL), w_up[l].astype(BF16), w_down[l].astype(BF16))

        kv_s = lambda t: t[row_s:row_s + n_sample_rows].reshape(n_dec, t_dec, SB_HEADS, SB_HEAD_DIM)
        for lst, val in zip(outs, (kt, vt, C_p, n_p, m_p, kv_s(sk), kv_s(sv), C_s, n_s, m_s)):
            lst.append(val)

    y_prompt = x[:row_s].reshape(n_batch, tp, D_MODEL)[:, N_META:t_prompt]
    y_sample = x[row_s:row_s + n_sample_rows].reshape(n_dec, t_dec, D_MODEL)
    outs = [jnp.stack(o) for o in outs]
    outs[0], outs[1] = (jnp.transpose(t, (0, 1, 4, 2, 3)) for t in outs[:2])
    outs[2], outs[7] = jnp.swapaxes(outs[2], -1, -2), jnp.swapaxes(outs[7], -1, -2)
    return (y_prompt, y_sample) + tuple(outs)
```
